```python
import math, functools
import jax, jax.numpy as jnp
from jax import lax
import numpy as np

D_MODEL = 2048
BATCH = 1
SEQ = 8192
DEPTH = 2
DEC_BATCH = 128
DEC_SEQ = 4
PAST_LEN = 2048
PAGE_SIZE = 128

N_A_LAYERS = DEPTH // 2
N_B_LAYERS = DEPTH - N_A_LAYERS
RET_HEADS = 8
RET_DK = D_MODEL // RET_HEADS
RET_DV = 2 * D_MODEL // RET_HEADS
RET_CHUNK = 128
RET_IN_WIDTH = 2 * RET_HEADS * RET_DK + 2 * RET_HEADS * RET_DV
ROPE_BASE = 10000.0
FOX_HEADS = 16
FOX_DH = D_MODEL // FOX_HEADS
Q_BLOCK = 128
KVF_WIDTH = 2 * FOX_HEADS * FOX_DH + FOX_HEADS
D_FF = 4 * D_MODEL
EPS = 1e-6

kernel_name = "yoco_retention_fox_decoder_step"


def rmsnorm(x, g):
    xf = x.astype(jnp.float32)
    y = xf * lax.rsqrt(jnp.mean(xf * xf, axis=-1, keepdims=True) + EPS)
    return (y * g.astype(jnp.float32)).astype(x.dtype)


def squared_relu_mlp(h, w_up, w_down):
    a = jnp.maximum(h @ w_up, 0)
    return (a * a) @ w_down


def rotary(x, pos):
    half = x.shape[-1] // 2
    inv = 1.0 / (ROPE_BASE ** (jnp.arange(half, dtype=jnp.float32) / half))
    ang = pos.astype(jnp.float32)[:, None] * inv[None, :]
    cos = jnp.cos(ang)[None, :, None, :]
    sin = jnp.sin(ang)[None, :, None, :]
    x1 = x[..., :half].astype(jnp.float32)
    x2 = x[..., half:].astype(jnp.float32)
    return jnp.concatenate([x1 * cos - x2 * sin, x1 * sin + x2 * cos], axis=-1)


def retention_log_gamma():
    return jnp.log1p(-(2.0 ** (-5.0 - jnp.arange(RET_HEADS, dtype=jnp.float32))))


def retention_chunk(S, q, k, v, log_gamma):
    C = q.shape[1]
    idx = jnp.arange(C, dtype=jnp.float32)
    diff = idx[:, None] - idx[None, :]
    decay = jnp.where(diff >= 0, jnp.exp(log_gamma[:, None, None] * jnp.maximum(diff, 0.0)), 0.0)
    vf = v.astype(jnp.float32)
    scores = jnp.einsum('bihd,bjhd->bhij', q, k) * decay[None]
    inner = jnp.einsum('bhij,bjhe->bihe', scores, vf)
    cross_scale = jnp.exp(log_gamma[None, :] * (idx[:, None] + 1.0))
    cross = jnp.einsum('bihd,bhde->bihe', q, S) * cross_scale[None, :, :, None]
    k_scale = jnp.exp(log_gamma[None, :] * (C - 1.0 - idx[:, None]))
    S_new = (jnp.exp(log_gamma * C)[None, :, None, None] * S
             + jnp.einsum('bjhd,bjhe->bhde', k * k_scale[None, :, :, None], vf))
    return S_new, inner + cross


def retention_mix(h, S0, pos, w_in, norm_g, w_o):
    B, T, _ = h.shape
    proj = h @ w_in
    q, k, v, g = jnp.split(proj, [RET_HEADS * RET_DK, 2 * RET_HEADS * RET_DK,
                                  2 * RET_HEADS * RET_DK + RET_HEADS * RET_DV], axis=-1)
    q = rotary(q.reshape(B, T, RET_HEADS, RET_DK), pos)
    k = rotary(k.reshape(B, T, RET_HEADS, RET_DK), pos) * (RET_DK ** -0.5)
    v = v.reshape(B, T, RET_HEADS, RET_DV)
    c = RET_CHUNK if T % RET_CHUNK == 0 else T
    n = T // c

    def to_chunks(a):
        return a.reshape(B, n, c, *a.shape[2:]).swapaxes(0, 1)

    log_gamma = retention_log_gamma()

    def step(S, qkv):
        qc, kc, vc = qkv
        return retention_chunk(S, qc, kc, vc, log_gamma)

    S_final, o = lax.scan(step, S0.astype(jnp.float32), (to_chunks(q), to_chunks(k), to_chunks(v)))
    o = o.swapaxes(0, 1).reshape(B, T, RET_HEADS, RET_DV)
    o = o * lax.rsqrt(jnp.mean(o * o, axis=-1, keepdims=True) + EPS)
    o = o.reshape(B, T, RET_HEADS * RET_DV) * norm_g.astype(jnp.float32)
    y = (jax.nn.silu(g.astype(jnp.float32)) * o).astype(h.dtype)
    return y @ w_o, S_final


def shared_kv(x, norm_g, w_kvf, b_f):
    B, T, _ = x.shape
    p = rmsnorm(x, norm_g) @ w_kvf
    k, v, fl = jnp.split(p, [FOX_HEADS * FOX_DH, 2 * FOX_HEADS * FOX_DH], axis=-1)
    k = k.reshape(B, T, FOX_HEADS, FOX_DH)
    v = v.reshape(B, T, FOX_HEADS, FOX_DH)
    logf = jax.nn.log_sigmoid(fl.astype(jnp.float32) + b_f.astype(jnp.float32))
    return k, v, logf


def fox_attend(q, k, v, cum_q, cum_k, q_pos, k_pos):
    s = jnp.einsum('bqhd,bkhd->bhqk', q.astype(jnp.float32), k) * (FOX_DH ** -0.5)
    s = s + jnp.transpose(cum_q, (0, 2, 1))[..., None] - jnp.transpose(cum_k, (0, 2, 1))[:, :, None, :]
    mask = k_pos[None, :] <= q_pos[:, None]
    s = jnp.where(mask[None, None], s, -jnp.inf)
    p = jax.nn.softmax(s, axis=-1)
    return jnp.einsum('bhqk,bkhd->bqhd', p, v)


def fox_prompt(q, k, v, logf):
    B, T, H, DH = q.shape
    cum = jnp.cumsum(logf, axis=1)
    kf = k.astype(jnp.float32)
    vf = v.astype(jnp.float32)
    nb = T // Q_BLOCK
    qb = q.reshape(B, nb, Q_BLOCK, H, DH).swapaxes(0, 1)
    cb = cum.reshape(B, nb, Q_BLOCK, H).swapaxes(0, 1)
    pb = jnp.arange(T, dtype=jnp.int32).reshape(nb, Q_BLOCK)
    k_pos = jnp.arange(T, dtype=jnp.int32)

    def block(args):
        qi, ci, pi = args
        return fox_attend(qi, kf, vf, ci, cum, pi, k_pos)

    out = lax.map(block, (qb, cb, pb))
    return out.swapaxes(0, 1).reshape(B, T, H * DH)


def fox_sample(q, k_new, v_new, logf_new, cache_k, cache_v, cache_logf, page_table):
    Bd, T, H, DH = q.shape
    past = page_table.shape[1] * PAGE_SIZE
    k_all = jnp.concatenate([cache_k[page_table].reshape(Bd, past, H, DH).astype(jnp.float32),
                             k_new.astype(jnp.float32)], axis=1)
    v_all = jnp.concatenate([cache_v[page_table].reshape(Bd, past, H, DH).astype(jnp.float32),
                             v_new.astype(jnp.float32)], axis=1)
    logf_all = jnp.concatenate([cache_logf[page_table].reshape(Bd, past, H).astype(jnp.float32),
                                logf_new.astype(jnp.float32)], axis=1)
    cum = jnp.cumsum(logf_all, axis=1)
    q_pos = past + jnp.arange(T, dtype=jnp.int32)
    k_pos = jnp.arange(past + T, dtype=jnp.int32)
    out = fox_attend(q, k_all, v_all, cum[:, past:], cum, q_pos, k_pos)
    return out.reshape(Bd, T, H * DH)


def run_group(x, pos, ret_states, attend, p):
    B, T, _ = x.shape
    new_states = []
    k = v = logf = None
    for layer in range(DEPTH):
        if layer < N_A_LAYERS:
            h = rmsnorm(x, p['norm_a_g'][layer])
            y, s_new = retention_mix(h, ret_states[layer], pos, p['w_ret_in'][layer],
                                     p['ret_norm_g'][layer], p['w_ret_o'][layer])
            x = x + y
            new_states.append(s_new)
        else:
            if layer == N_A_LAYERS:
                k, v, logf = shared_kv(x, p['norm_kv_g'], p['w_kvf'], p['b_f'])
            j = layer - N_A_LAYERS
            q = (rmsnorm(x, p['norm_b_g'][j]) @ p['w_fox_q'][j]).reshape(B, T, FOX_HEADS, FOX_DH)
            x = x + attend(q, k, v, logf).astype(x.dtype) @ p['w_fox_o'][j]
        x = x + squared_relu_mlp(rmsnorm(x, p['norm_mlp_g'][layer]), p['w_mlp_up'][layer], p['w_mlp_down'][layer])
    return rmsnorm(x, p['norm_f_g']), jnp.stack(new_states), k, v, logf


def setup_inputs(seed: int = 0) -> dict:
    key = jax.random.key(seed)
    ks = jax.random.split(key, 24)
    f32 = jnp.float32
    n_pages = PAST_LEN // PAGE_SIZE
    n_used = DEC_BATCH * n_pages
    n_phys = n_used + (n_used + 3) // 4
    page_table = jax.random.permutation(ks[0], n_phys)[:n_used].reshape(DEC_BATCH, n_pages).astype(jnp.int32)

    def nrm(k, shape, scale):
        return jax.random.normal(k, shape, f32) * scale

    def gain(k, shape):
        return 1.0 + 0.05 * jax.random.normal(k, shape, f32)

    return {
        'x_prompt': nrm(ks[1], (BATCH, SEQ, D_MODEL), 1.0),
        'x_sample': nrm(ks[2], (DEC_BATCH, DEC_SEQ, D_MODEL), 1.0),
        'state_ret': nrm(ks[3], (N_A_LAYERS, DEC_BATCH, RET_HEADS, RET_DK, RET_DV), 0.1),
        'cache_k': nrm(ks[4], (n_phys, PAGE_SIZE, FOX_HEADS, FOX_DH), 1.0),
        'cache_v': nrm(ks[5], (n_phys, PAGE_SIZE, FOX_HEADS, FOX_DH), 1.0),
        'cache_logf': jax.nn.log_sigmoid(3.0 + jax.random.normal(ks[6], (n_phys, PAGE_SIZE, FOX_HEADS), f32)),
        'page_table': page_table,
        'norm_a_g': gain(ks[7], (N_A_LAYERS, D_MODEL)),
        'w_ret_in': nrm(ks[8], (N_A_LAYERS, D_MODEL, RET_IN_WIDTH), D_MODEL ** -0.5),
        'ret_norm_g': gain(ks[9], (N_A_LAYERS, RET_HEADS * RET_DV)),
        'w_ret_o': nrm(ks[10], (N_A_LAYERS, RET_HEADS * RET_DV, D_MODEL), (RET_HEADS * RET_DV) ** -0.5),
        'norm_kv_g': gain(ks[11], (D_MODEL,)),
        'w_kvf': nrm(ks[12], (D_MODEL, KVF_WIDTH), D_MODEL ** -0.5),
        'b_f': jnp.linspace(1.0, 6.0, FOX_HEADS, dtype=f32) + 0.1 * jax.random.normal(ks[13], (FOX_HEADS,), f32),
        'norm_b_g': gain(ks[14], (N_B_LAYERS, D_MODEL)),
        'w_fox_q': nrm(ks[15], (N_B_LAYERS, D_MODEL, FOX_HEADS * FOX_DH), D_MODEL ** -0.5),
        'w_fox_o': nrm(ks[16], (N_B_LAYERS, FOX_HEADS * FOX_DH, D_MODEL), (FOX_HEADS * FOX_DH) ** -0.5),
        'norm_mlp_g': gain(ks[17], (DEPTH, D_MODEL)),
        'w_mlp_up': nrm(ks[18], (DEPTH, D_MODEL, D_FF), D_MODEL ** -0.5),
        'w_mlp_down': nrm(ks[19], (DEPTH, D_FF, D_MODEL), D_FF ** -0.5),
        'norm_f_g': gain(ks[20], (D_MODEL,)),
    }


def reference(x_prompt, x_sample, state_ret, cache_k, cache_v, cache_logf, page_table,
              norm_a_g, w_ret_in, ret_norm_g, w_ret_o, norm_kv_g, w_kvf, b_f,
              norm_b_g, w_fox_q, w_fox_o, norm_mlp_g, w_mlp_up, w_mlp_down, norm_f_g):
    p = dict(norm_a_g=norm_a_g, w_ret_in=w_ret_in, ret_norm_g=ret_norm_g, w_ret_o=w_ret_o,
             norm_kv_g=norm_kv_g, w_kvf=w_kvf, b_f=b_f, norm_b_g=norm_b_g, w_fox_q=w_fox_q,
             w_fox_o=w_fox_o, norm_mlp_g=norm_mlp_g, w_mlp_up=w_mlp_up, w_mlp_down=w_mlp_down,
             norm_f_g=norm_f_g)
    B, T = x_prompt.shape[:2]
    Bd, Td = x_sample.shape[:2]
    past = page_table.shape[1] * PAGE_SIZE

    zero_states = jnp.zeros((N_A_LAYERS, B, RET_HEADS, RET_DK, RET_DV), jnp.float32)
    pos_p = jnp.arange(T, dtype=jnp.int32)
    y_prompt, ret_state_prompt, k_prompt, v_prompt, logf_prompt = run_group(
        x_prompt, pos_p, zero_states, fox_prompt, p)

    pos_s = past + jnp.arange(Td, dtype=jnp.int32)

    def attend_sample(q, k, v, logf):
        return fox_sample(q, k, v, logf, cache_k, cache_v, cache_logf, page_table)

    y_sample, ret_state_sample, k_sample, v_sample, logf_sample = run_group(
        x_sample, pos_s, state_ret, attend_sample, p)

    return (y_prompt, y_sample, ret_state_prompt, k_prompt, v_prompt, logf_prompt,
            ret_state_sample, k_sample, v_sample, logf_sample)
```

```python
import functools
import math

import jax
import jax.numpy as jnp
from jax import lax
from jax.experimental import pallas as pl
from jax.experimental.pallas import tpu as pltpu

F32 = jnp.float32
BF16 = jnp.bfloat16

EPS = 1e-6
ROPE_BASE = 10000.0
RET_HEADS = 8
RET_CHUNK = 128
FOX_HEADS = 16
PAGE_SIZE = 128
LANES = 128
SAMPLE_ROWS = 8
VMEM_LIMIT = 56 * 1024 * 1024


def _cparams(*semantics):
    return pltpu.CompilerParams(dimension_semantics=semantics, vmem_limit_bytes=VMEM_LIMIT)


def _dot(a, b):
    return jnp.dot(a, b, preferred_element_type=F32)


def _dot_nt(a, b):
    return lax.dot_general(a, b, (((1,), (1,)), ((), ())), preferred_element_type=F32)


def _dot_tn(a, b):
    return lax.dot_general(a, b, (((0,), (0,)), ((), ())), preferred_element_type=F32)


def _log2(n):
    assert n > 0 and n & (n - 1) == 0, n
    return n.bit_length() - 1


def _rms_unit(x):
    return x * lax.rsqrt(jnp.mean(x * x, axis=-1, keepdims=True) + EPS)


def _norm_proj_kernel(*refs, mode, rope_tiles, kscale_from):
    if mode == "rope":
        x_ref, g_ref, w_ref, cos_ref, sin_ref, o_ref, h_scr = refs
    elif mode == "logsig":
        x_ref, g_ref, w_ref, b_ref, o_ref, h_scr = refs
    elif mode == "dual":
        x_ref, g_ref, w_ref, o_ref, o16_ref, h_scr = refs
    else:
        x_ref, g_ref, w_ref, o_ref, h_scr = refs
    j = pl.program_id(1)

    @pl.when(j == 0)
    def _():
        h_scr[...] = (_rms_unit(x_ref[...]) * g_ref[...]).astype(BF16)

    acc = _dot(h_scr[...], w_ref[...])

    if mode == "rope":
        @pl.when(j < rope_tiles)
        def _():
            cos = cos_ref[...]
            sin = sin_ref[...]
            half = cos.shape[1]
            scale = jnp.where(j >= kscale_from, (2 * half) ** -0.5, 1.0).astype(F32)
            for hh in range(acc.shape[1] // (2 * half)):
                lo = hh * 2 * half
                x1 = acc[:, lo:lo + half]
                x2 = acc[:, lo + half:lo + 2 * half]
                o_ref[:, lo:lo + half] = (x1 * cos - x2 * sin) * scale
                o_ref[:, lo + half:lo + 2 * half] = (x1 * sin + x2 * cos) * scale

        @pl.when(j >= rope_tiles)
        def _():
            o_ref[...] = acc
    elif mode == "logsig":
        z = acc + b_ref[...]
        o_ref[...] = jnp.minimum(z, 0.0) - jnp.log1p(jnp.exp(-jnp.abs(z)))
    elif mode == "dual":
        o_ref[...] = acc
        o16_ref[...] = acc.astype(BF16)
    else:
        o_ref[...] = acc.astype(o_ref.dtype)


def _norm_proj(x, g, w, *, mode="plain", out_dtype=F32, tm, tn, row_blocks=None,
               cos=None, sin=None, bias=None, rope_tiles=0, kscale_from=0):
    m, k = x.shape
    n = w.shape[1]
    r0, nr = row_blocks if row_blocks is not None else (0, m // tm)
    tn = min(tn, n)
    assert m % tm == 0 and n % tn == 0
    grid = (nr, n // tn)
    row = lambda i, j: (i + r0, 0)
    in_specs = [pl.BlockSpec((tm, k), row),
                pl.BlockSpec((1, k), lambda i, j: (0, 0)),
                pl.BlockSpec((k, tn), lambda i, j: (0, j))]
    args = [x, g.reshape(1, k), w]
    if mode == "rope":
        in_specs += [pl.BlockSpec((tm, cos.shape[1]), row)] * 2
        args += [cos, sin]
    if mode == "logsig":
        in_specs += [pl.BlockSpec((1, tn), lambda i, j: (0, j))]
        args += [bias]
    out_spec = pl.BlockSpec((tm, tn), lambda i, j: (i, j))
    out_shape = jax.ShapeDtypeStruct((nr * tm, n), out_dtype)
    if mode == "dual":
        out_shape = (out_shape, jax.ShapeDtypeStruct((nr * tm, n), BF16))
        out_spec = (out_spec, out_spec)
    return pl.pallas_call(
        functools.partial(_norm_proj_kernel, mode=mode, rope_tiles=rope_tiles,
                          kscale_from=kscale_from),
        out_shape=out_shape, grid=grid, in_specs=in_specs, out_specs=out_spec,
        scratch_shapes=[pltpu.VMEM((tm, k), BF16)],
        compiler_params=_cparams("parallel", "arbitrary"),
        name="norm_proj_" + mode,
    )(*args)


def _proj_res_kernel(a_ref, w_ref, r_ref, o_ref):
    o_ref[...] = r_ref[...] + _dot(a_ref[...], w_ref[...])


def _proj_res(a, w, res, *, tm, tn):
    m, k = a.shape
    n = w.shape[1]
    tn = min(tn, n)
    assert m % tm == 0 and n % tn == 0
    return pl.pallas_call(
        _proj_res_kernel,
        out_shape=jax.ShapeDtypeStruct((m, n), F32),
        grid=(m // tm, n // tn),
        in_specs=[pl.BlockSpec((tm, k), lambda i, j: (i, 0)),
                  pl.BlockSpec((k, tn), lambda i, j: (0, j)),
                  pl.BlockSpec((tm, tn), lambda i, j: (i, j))],
        out_specs=pl.BlockSpec((tm, tn), lambda i, j: (i, j)),
        compiler_params=_cparams("parallel", "arbitrary"),
        name="proj_res",
    )(a, w, res)


def _mlp_kernel(*refs, final_norm):
    if final_norm:
        x_ref, g_ref, wu_ref, wd_ref, gf_ref, o_ref, h_scr, acc_scr = refs
    else:
        x_ref, g_ref, wu_ref, wd_ref, o_ref, h_scr, acc_scr = refs
    f = pl.program_id(1)

    @pl.when(f == 0)
    def _():
        h_scr[...] = (_rms_unit(x_ref[...]) * g_ref[...]).astype(BF16)
        acc_scr[...] = jnp.zeros_like(acc_scr)

    a = jnp.maximum(_dot(h_scr[...], wu_ref[...]), 0.0)
    acc_scr[...] += _dot((a * a).astype(BF16), wd_ref[...])

    @pl.when(f == pl.num_programs(1) - 1)
    def _():
        y = x_ref[...] + acc_scr[...]
        if final_norm:
            y = _rms_unit(y) * gf_ref[...]
        o_ref[...] = y


def _mlp(x, g, w_up, w_down, *, tm, tf, row_blocks=None, final_g=None):
    m, d = x.shape
    ff = w_up.shape[1]
    r0, nr = row_blocks if row_blocks is not None else (0, m // tm)
    assert m % tm == 0 and ff % tf == 0
    row = lambda i, f: (i + r0, 0)
    const = lambda i, f: (0, 0)
    in_specs = [pl.BlockSpec((tm, d), row),
                pl.BlockSpec((1, d), const),
                pl.BlockSpec((d, tf), lambda i, f: (0, f)),
                pl.BlockSpec((tf, d), lambda i, f: (f, 0))]
    args = [x, g.reshape(1, d), w_up, w_down]
    if final_g is not None:
        in_specs.append(pl.BlockSpec((1, d), const))
        args.append(final_g.reshape(1, d))
    return pl.pallas_call(
        functools.partial(_mlp_kernel, final_norm=final_g is not None),
        out_shape=jax.ShapeDtypeStruct((nr * tm, d), F32),
        grid=(nr, ff // tf),
        in_specs=in_specs,
        out_specs=pl.BlockSpec((tm, d), lambda i, f: (i, 0)),
        scratch_shapes=[pltpu.VMEM((tm, d), BF16), pltpu.VMEM((tm, d), F32)],
        compiler_params=_cparams("parallel", "arbitrary"),
        name="mlp",
    )(*args)


def _head_out(o, gate, ng):
    o = _rms_unit(o) * ng
    return gate / (1.0 + jnp.exp(-gate)) * o


def _ret_prompt_kernel(lg_ref, q_ref, k_ref, v_ref, g_ref, ng_ref, y_ref, s_ref, s_scr):
    h = pl.program_id(0)
    c = pl.program_id(1)
    chunk = q_ref.shape[0]
    lg = lg_ref[h]

    @pl.when(c == 0)
    def _():
        s_scr[...] = jnp.zeros_like(s_scr)

    q = q_ref[...].astype(BF16)
    k = k_ref[...]
    v = v_ref[...].astype(BF16)
    state = s_scr[...]

    ii = lax.broadcasted_iota(jnp.int32, (chunk, chunk), 0)
    jj = lax.broadcasted_iota(jnp.int32, (chunk, chunk), 1)
    diff = (ii - jj).astype(F32)
    decay = jnp.where(diff >= 0, jnp.exp(lg * jnp.maximum(diff, 0.0)), 0.0)
    idx = lax.broadcasted_iota(jnp.int32, (chunk, 1), 0).astype(F32)

    scores = _dot_nt(q, k.astype(BF16)) * decay
    inner = _dot(scores.astype(BF16), v)
    cross = _dot(q, state.astype(BF16)) * jnp.exp(lg * (idx + 1.0))
    k_scaled = (k * jnp.exp(lg * (chunk - 1.0 - idx))).astype(BF16)
    gamma_c = jnp.exp(lg * jnp.full((1, state.shape[1]), float(chunk), F32))
    s_scr[...] = gamma_c * state + _dot_tn(k_scaled, v)

    y_ref[...] = _head_out(inner + cross, g_ref[...], ng_ref[...]).astype(y_ref.dtype)

    @pl.when(c == pl.num_programs(1) - 1)
    def _():
        s_ref[0, 0, 0] = s_scr[...]


def _ret_prompt(proj, log_gamma, norm_g, *, t, dk, dv):
    nh = RET_HEADS
    chunk = RET_CHUNK if t % RET_CHUNK == 0 else t
    kb = nh * dk // dk
    vb = 2 * nh * dk // dv
    gb = vb + nh
    return pl.pallas_call(
        _ret_prompt_kernel,
        out_shape=(jax.ShapeDtypeStruct((t, nh * dv), BF16),
                   jax.ShapeDtypeStruct((1, 1, nh, dk, dv), F32)),
        grid_spec=pltpu.PrefetchScalarGridSpec(
            num_scalar_prefetch=0,
            grid=(nh, t // chunk),
            in_specs=[pl.BlockSpec(memory_space=pltpu.SMEM),
                      pl.BlockSpec((chunk, dk), lambda h, c: (c, h)),
                      pl.BlockSpec((chunk, dk), lambda h, c: (c, kb + h)),
                      pl.BlockSpec((chunk, dv), lambda h, c: (c, vb + h)),
                      pl.BlockSpec((chunk, dv), lambda h, c: (c, gb + h)),
                      pl.BlockSpec((1, dv), lambda h, c: (0, h))],
            out_specs=(pl.BlockSpec((chunk, dv), lambda h, c: (c, h)),
                       pl.BlockSpec((1, 1, 1, dk, dv), lambda h, c: (0, 0, h, 0, 0))),
            scratch_shapes=[pltpu.VMEM((dk, dv), F32)]),
        compiler_params=_cparams("parallel", "arbitrary"),
        name="ret_prompt",
    )(log_gamma, proj, proj, proj, proj, norm_g.reshape(1, nh * dv))


def _ret_sample_kernel(lg_ref, q_ref, k_ref, v_ref, g_ref, ng_ref, s0_ref, y_ref, s1_ref,
                       kpad_scr, vpad_scr, *, n_tok, dk, dv):
    rows = q_ref.shape[0]
    kpad_scr[...] = jnp.zeros_like(kpad_scr)
    vpad_scr[...] = jnp.zeros_like(vpad_scr)
    idx = lax.broadcasted_iota(jnp.int32, (rows, 1), 0).astype(F32)
    for h in range(RET_HEADS):
        lg = lg_ref[h]
        q = q_ref[:, h * dk:(h + 1) * dk]
        k = k_ref[:, h * dk:(h + 1) * dk]
        v = v_ref[:, h * dv:(h + 1) * dv]
        state = s0_ref[0, 0, h]

        cross = _dot(q.astype(BF16), state.astype(BF16)) * jnp.exp(lg * (idx + 1.0))
        inner = jnp.zeros((rows, dv), F32)
        for j in range(n_tok):
            s_j = jnp.sum(q * k[j:j + 1, :], axis=1, keepdims=True)
            d_j = jnp.where(idx >= j, jnp.exp(lg * jnp.maximum(idx - j, 0.0)), 0.0)
            inner = inner + (s_j * d_j) * v[j:j + 1, :]

        kpad_scr[0:rows, :] = k * jnp.exp(lg * (n_tok - 1.0 - idx))
        vpad_scr[0:rows, :] = v
        upd = _dot_tn(kpad_scr[...].astype(BF16), vpad_scr[...].astype(BF16))
        gamma_c = jnp.exp(lg * jnp.full((1, dv), float(n_tok), F32))
        s1_ref[0, 0, h] = gamma_c * state + upd

        y_ref[:, h * dv:(h + 1) * dv] = _head_out(
            inner + cross, g_ref[:, h * dv:(h + 1) * dv], ng_ref[:, h * dv:(h + 1) * dv])


def _ret_sample(proj, state, log_gamma, norm_g, *, row0, n_seq, n_tok, dk, dv):
    nh = RET_HEADS
    rows = SAMPLE_ROWS
    rb = row0 // rows
    qw = nh * dk
    vw = nh * dv
    return pl.pallas_call(
        functools.partial(_ret_sample_kernel, n_tok=n_tok, dk=dk, dv=dv),
        out_shape=(jax.ShapeDtypeStruct((n_seq * rows, vw), F32),
                   jax.ShapeDtypeStruct(state.shape, F32)),
        grid_spec=pltpu.PrefetchScalarGridSpec(
            num_scalar_prefetch=0,
            grid=(n_seq,),
            in_specs=[pl.BlockSpec(memory_space=pltpu.SMEM),
                      pl.BlockSpec((rows, qw), lambda b: (rb + b, 0)),
                      pl.BlockSpec((rows, qw), lambda b: (rb + b, 1)),
                      pl.BlockSpec((rows, vw), lambda b: (rb + b, 2 * qw // vw)),
                      pl.BlockSpec((rows, vw), lambda b: (rb + b, 2 * qw // vw + 1)),
                      pl.BlockSpec((1, vw), lambda b: (0, 0)),
                      pl.BlockSpec((1, 1, nh, dk, dv), lambda b: (0, b, 0, 0, 0))],
            out_specs=(pl.BlockSpec((rows, vw), lambda b: (b, 0)),
                       pl.BlockSpec((1, 1, nh, dk, dv), lambda b: (0, b, 0, 0, 0))),
            scratch_shapes=[pltpu.VMEM((LANES, dk), F32), pltpu.VMEM((LANES, dv), F32)]),
        compiler_params=_cparams("parallel"),
        name="ret_sample",
    )(log_gamma, proj, proj, proj, proj, norm_g.reshape(1, vw), state)


def _cumsum_kernel(x_ref, o_ref, carry_scr):
    @pl.when(pl.program_id(0) == 0)
    def _():
        carry_scr[...] = jnp.zeros_like(carry_scr)

    x = x_ref[...]
    n = x.shape[0]
    tri = (lax.broadcasted_iota(jnp.int32, (n, n), 0)
           >= lax.broadcasted_iota(jnp.int32, (n, n), 1)).astype(BF16)
    hi = x.astype(BF16)
    r1 = x - hi.astype(F32)
    mid = r1.astype(BF16)
    lo = (r1 - mid.astype(F32)).astype(BF16)
    c = _dot(tri, hi) + _dot(tri, mid) + _dot(tri, lo) + carry_scr[...]
    o_ref[...] = c
    carry_scr[...] = c[n - 1:n, :]


def _cumsum_rows(x, *, rows, tb):
    return pl.pallas_call(
        _cumsum_kernel,
        out_shape=jax.ShapeDtypeStruct((rows, x.shape[1]), F32),
        grid=(rows // tb,),
        in_specs=[pl.BlockSpec((tb, x.shape[1]), lambda i: (i, 0))],
        out_specs=pl.BlockSpec((tb, x.shape[1]), lambda i: (i, 0)),
        scratch_shapes=[pltpu.VMEM((1, x.shape[1]), F32)],
        compiler_params=_cparams("arbitrary"),
        name="cumsum",
    )(x)


def _flash_kernel(q_ref, k_ref, v_ref, cq_ref, ck_ref, o_ref, *, scale):
    i = pl.program_id(1)
    tq, dh = q_ref.shape
    tk = ck_ref.shape[3]
    q = q_ref[...].astype(BF16)
    cq_row = cq_ref[0, 0]
    cq = jnp.broadcast_to(cq_row, (LANES, tq)).T
    cq = jnp.concatenate([cq] * (tk // LANES), axis=1)

    def step(j, carry, masked):
        m, l, acc = carry
        start = pl.multiple_of(j * tk, tk)
        k = k_ref[pl.ds(start, tk), :]
        v = v_ref[pl.ds(start, tk), :]
        s = _dot_nt(q, k) * scale + cq - ck_ref[0, j]
        if masked:
            qpos = i * tq + lax.broadcasted_iota(jnp.int32, (tq, tk), 0)
            kpos = j * tk + lax.broadcasted_iota(jnp.int32, (tq, tk), 1)
            s = jnp.where(kpos <= qpos, s, -jnp.inf)
        m_new = jnp.maximum(m, jnp.max(s, axis=1, keepdims=True))
        p = jnp.exp(s - m_new)
        alpha = jnp.exp(m - m_new)
        l = alpha * l + jnp.sum(p, axis=1, keepdims=True)
        acc = alpha * acc + _dot(p.astype(BF16), v)
        return m_new, l, acc

    init = (jnp.full((tq, 1), -jnp.inf, F32), jnp.zeros((tq, 1), F32), jnp.zeros((tq, dh), F32))
    carry = lax.fori_loop(0, i, lambda j, c: step(j, c, False), init)
    m, l, acc = step(i, carry, True)
    o_ref[...] = (acc / l).astype(o_ref.dtype)


def _fox_prompt(q, kv16, cum_t, *, t, dh, tq):
    nh = FOX_HEADS
    cum_blk = cum_t.reshape(nh, t // tq, 1, tq)
    return pl.pallas_call(
        functools.partial(_flash_kernel, scale=dh ** -0.5),
        out_shape=jax.ShapeDtypeStruct((t, nh * dh), BF16),
        grid=(nh, t // tq),
        in_specs=[pl.BlockSpec((tq, dh), lambda h, i: (i, h)),
                  pl.BlockSpec((t, dh), lambda h, i: (0, h)),
                  pl.BlockSpec((t, dh), lambda h, i: (0, nh + h)),
                  pl.BlockSpec((1, 1, 1, tq), lambda h, i: (h, i, 0, 0)),
                  pl.BlockSpec((1, t // tq, 1, tq), lambda h, i: (h, 0, 0, 0))],
        out_specs=pl.BlockSpec((tq, dh), lambda h, i: (i, h)),
        compiler_params=_cparams("parallel", "arbitrary"),
        name="fox_prompt",
    )(q, kv16, kv16, cum_blk, cum_blk)


def _suffix_scan(x):
    n = x.shape[1]
    lane = lax.broadcasted_iota(jnp.int32, x.shape, 1)
    y = x
    s = 1
    while s < n:
        shifted = pltpu.roll(y, n - s, axis=1)
        y = y + jnp.where(lane + s < n, shifted, 0.0)
        s *= 2
    return y


def _decode_kernel(pt_ref, q_ref, kn_ref, vn_ref, lfn_ref, *refs, n_tok, pages_per_step, dh,
                   scale):
    del pt_ref
    pps = pages_per_step
    k_refs = refs[0:pps]
    v_refs = refs[pps:2 * pps]
    lf_refs = refs[2 * pps:3 * pps]
    o_ref, qbd_scr, m_scr, l_scr, acc_scr, carry_scr = refs[3 * pps:]
    nh = FOX_HEADS
    nq = n_tok * nh
    width = nh * dh
    step = pl.program_id(1)

    row = lax.broadcasted_iota(jnp.int32, (nq, width), 0)
    col = lax.broadcasted_iota(jnp.int32, (nq, width), 1)
    head_of_row = jnp.bitwise_and(row, nh - 1)
    diag = head_of_row == lax.shift_right_logical(col, _log2(dh))

    @pl.when(step == 0)
    def _():
        q = q_ref[...].astype(F32)
        qrep = jnp.concatenate(
            [jnp.broadcast_to(q[t:t + 1, :], (nh, width)) for t in range(n_tok)], axis=0)
        qbd = jnp.where(diag, qrep, 0.0)
        qbd_scr[...] = qbd.astype(BF16)

        kn = kn_ref[...]
        vn = vn_ref[...]
        lfn = lfn_ref[...]
        r1 = lax.broadcasted_iota(jnp.int32, (nq, LANES), 0)
        c1 = lax.broadcasted_iota(jnp.int32, (nq, LANES), 1)
        onehot = (c1 == jnp.bitwise_and(r1, nh - 1)).astype(F32)
        tq_col = lax.shift_right_logical(lax.broadcasted_iota(jnp.int32, (nq, 1), 0), _log2(nh))
        cn = jnp.zeros((1, LANES), F32)
        s_new = []
        for t in range(n_tok):
            cn = cn + lfn[t:t + 1, :]
            bias = jnp.sum(onehot * cn, axis=1, keepdims=True)
            s_t = jnp.sum(qbd * kn[t:t + 1, :], axis=1, keepdims=True) * scale - bias
            s_new.append(jnp.where(tq_col >= t, s_t, -jnp.inf))
        m0 = s_new[0]
        for t in range(1, n_tok):
            m0 = jnp.maximum(m0, s_new[t])
        l0 = jnp.zeros((nq, 1), F32)
        acc0 = jnp.zeros((nq, width), F32)
        for t in range(n_tok):
            p_t = jnp.exp(s_new[t] - m0)
            l0 = l0 + p_t
            acc0 = acc0 + p_t * vn[t:t + 1, :]
        m_scr[...] = m0
        l_scr[...] = l0
        acc_scr[...] = acc0
        carry_scr[...] = jnp.zeros_like(carry_scr)

    qbd16 = qbd_scr[...]
    carry = carry_scr[...]
    s_pages = []
    for r in range(pps):
        lf = lf_refs[r][0]
        incl = _suffix_scan(lf)
        bias = incl - lf + carry
        carry = carry + incl[:, 0:1]
        bias = jnp.concatenate([bias] * n_tok, axis=0)
        k16 = k_refs[r][0].astype(BF16)
        s_pages.append(_dot_nt(qbd16, k16) * scale + bias)
    carry_scr[...] = carry

    m_prev = m_scr[...]
    m_new = m_prev
    for s in s_pages:
        m_new = jnp.maximum(m_new, jnp.max(s, axis=1, keepdims=True))
    alpha = jnp.exp(m_prev - m_new)
    l_new = alpha * l_scr[...]
    acc = alpha * acc_scr[...]
    for r in range(pps):
        p = jnp.exp(s_pages[r] - m_new)
        l_new = l_new + jnp.sum(p, axis=1, keepdims=True)
        acc = acc + _dot(p.astype(BF16), v_refs[r][0].astype(BF16))
    m_scr[...] = m_new
    l_scr[...] = l_new
    acc_scr[...] = acc

    @pl.when(step == pl.num_programs(1) - 1)
    def _():
        o = jnp.where(diag, acc / l_new, 0.0)
        for t in range(n_tok):
            o_ref[t:t + 1, :] = jnp.sum(o[t * nh:(t + 1) * nh, :], axis=0, keepdims=True)
        o_ref[n_tok:, :] = jnp.zeros((o_ref.shape[0] - n_tok, width), F32)


def _fox_sample(q, k_new, v_new, logf_new, cache_k, cache_v, cache_lf_t, page_table, *,
                row0, n_tok, dh, pages_per_step):
    nh = FOX_HEADS
    n_seq, n_pages = page_table.shape
    rows = SAMPLE_ROWS
    rb = row0 // rows
    width = nh * dh
    pps = pages_per_step
    assert n_pages % pps == 0
    nq = n_tok * nh

    def tok(b, s, pt):
        return (rb + b, 0)

    def page(r):
        return lambda b, s, pt: (pt[b, n_pages - 1 - (s * pps + r)], 0, 0)

    in_specs = [pl.BlockSpec((rows, width), tok)] * 3 + [pl.BlockSpec((rows, LANES), tok)]
    in_specs += [pl.BlockSpec((1, PAGE_SIZE, width), page(r)) for r in range(pps)] * 2
    in_specs += [pl.BlockSpec((1, nh, PAGE_SIZE), page(r)) for r in range(pps)]
    args = [q, k_new, v_new, logf_new] + [cache_k] * pps + [cache_v] * pps + [cache_lf_t] * pps
    return pl.pallas_call(
        functools.partial(_decode_kernel, n_tok=n_tok, pages_per_step=pps, dh=dh,
                          scale=dh ** -0.5),
        out_shape=jax.ShapeDtypeStruct((n_seq * rows, width), F32),
        grid_spec=pltpu.PrefetchScalarGridSpec(
            num_scalar_prefetch=1,
            grid=(n_seq, n_pages // pps),
            in_specs=in_specs,
            out_specs=pl.BlockSpec((rows, width), lambda b, s, pt: (b, 0)),
            scratch_shapes=[pltpu.VMEM((nq, width), BF16),
                            pltpu.VMEM((nq, 1), F32),
                            pltpu.VMEM((nq, 1), F32),
                            pltpu.VMEM((nq, width), F32),
                            pltpu.VMEM((nh, PAGE_SIZE), F32)]),
        compiler_params=_cparams("parallel", "arbitrary"),
        name="fox_sample",
    )(page_table, *args)


def kernel(x_prompt, x_sample, state_ret, cache_k, cache_v, cache_logf, page_table, norm_a_g,
           w_ret_in, ret_norm_g, w_ret_o, norm_kv_g, w_kvf, b_f, norm_b_g, w_fox_q, w_fox_o,
           norm_mlp_g, w_mlp_up, w_mlp_down, norm_f_g):
    n_b, t, d = x_prompt.shape
    n_seq, n_tok, _ = x_sample.shape
    assert n_b == 1 and n_tok <= SAMPLE_ROWS
    n_a = state_ret.shape[0]
    n_layers = w_mlp_up.shape[0]
    assert n_a == 1 and n_layers == 2
    dk, dv = state_ret.shape[3], state_ret.shape[4]
    nh_r = RET_HEADS
    nh_f = FOX_HEADS
    dh = d // nh_f
    past = page_table.shape[1] * PAGE_SIZE
    ms = n_seq * SAMPLE_ROWS
    tm = 512
    assert t % tm == 0 and ms % tm == 0
    p_blocks = (0, t // tm)
    s_blocks = (t // tm, ms // tm)

    xs = jnp.pad(x_sample, ((0, 0), (0, SAMPLE_ROWS - n_tok), (0, 0))).reshape(ms, d)
    x = jnp.concatenate([x_prompt.reshape(t, d), xs], axis=0)

    half = dk // 2
    pos = jnp.concatenate([jnp.arange(t, dtype=jnp.int32),
                           jnp.tile(past + jnp.arange(SAMPLE_ROWS, dtype=jnp.int32), n_seq)])
    inv = 1.0 / (ROPE_BASE ** (jnp.arange(half, dtype=F32) / half))
    ang = pos.astype(F32)[:, None] * inv[None, :]
    cos = jnp.cos(ang)
    sin = jnp.sin(ang)
    log_gamma = jnp.log1p(-(2.0 ** (-5.0 - jnp.arange(nh_r, dtype=F32))))

    tn = 512
    qk_w = 2 * nh_r * dk
    proj = _norm_proj(x, norm_a_g[0], w_ret_in[0].astype(BF16), mode="rope", tm=tm, tn=tn,
                      cos=cos, sin=sin, rope_tiles=qk_w // tn, kscale_from=qk_w // (2 * tn))
    y_p, ret_state_prompt = _ret_prompt(proj, log_gamma, ret_norm_g[0], t=t, dk=dk, dv=dv)
    y_s, ret_state_sample = _ret_sample(proj, state_ret, log_gamma, ret_norm_g[0], row0=t,
                                        n_seq=n_seq, n_tok=n_tok, dk=dk, dv=dv)
    y = jnp.concatenate([y_p, y_s.astype(BF16)], axis=0)
    x = _proj_res(y, w_ret_o[0].astype(BF16), x, tm=tm, tn=1024)
    x = _mlp(x, norm_mlp_g[0], w_mlp_up[0].astype(BF16), w_mlp_down[0].astype(BF16),
             tm=tm, tf=512)

    kv_w = 2 * nh_f * dh
    w_kv = w_kvf[:, :kv_w].astype(BF16)
    w_f = jnp.pad(w_kvf[:, kv_w:], ((0, 0), (0, LANES - nh_f))).astype(BF16)
    b_pad = jnp.pad(b_f, (0, LANES - nh_f)).reshape(1, LANES)
    kv_p, kv16_p = _norm_proj(x, norm_kv_g, w_kv, mode="dual", tm=tm, tn=1024,
                              row_blocks=p_blocks)
    kv_s = _norm_proj(x, norm_kv_g, w_kv, tm=tm, tn=1024, row_blocks=s_blocks)
    logf = _norm_proj(x, norm_kv_g, w_f, mode="logsig", bias=b_pad, tm=tm, tn=LANES)

    q = _norm_proj(x, norm_b_g[0], w_fox_q[0].astype(BF16), tm=tm, tn=1024)
    cum = _cumsum_rows(logf, rows=t, tb=512)
    cum_t = cum[:, :nh_f].T
    a_p = _fox_prompt(q, kv16_p, cum_t, t=t, dh=dh, tq=512)
    n_phys = cache_k.shape[0]
    a_s = _fox_sample(q[t:], kv_s[:, :nh_f * dh], kv_s[:, nh_f * dh:], logf[t:],
                      cache_k.reshape(n_phys, PAGE_SIZE, nh_f * dh),
                      cache_v.reshape(n_phys, PAGE_SIZE, nh_f * dh),
                      jnp.swapaxes(cache_logf, 1, 2), page_table,
                      row0=0, n_tok=n_tok, dh=dh, pages_per_step=min(4, page_table.shape[1]))
    a = jnp.concatenate([a_p, a_s.astype(BF16)], axis=0)
    x = _proj_res(a, w_fox_o[0].astype(BF16), x, tm=tm, tn=1024)
    wu1 = w_mlp_up[1].astype(BF16)
    wd1 = w_mlp_down[1].astype(BF16)
    y_prompt = _mlp(x, norm_mlp_g[1], wu1, wd1, tm=tm, tf=512, row_blocks=p_blocks,
                    final_g=norm_f_g)
    y_sample = _mlp(x, norm_mlp_g[1], wu1, wd1, tm=tm, tf=512, row_blocks=s_blocks,
                    final_g=norm_f_g)

    def sample_rows(z):
        return z.reshape(n_seq, SAMPLE_ROWS, -1)[:, :n_tok]

    kw = nh_f * dh
    return (y_prompt.reshape(1, t, d),
            sample_rows(y_sample),
            ret_state_prompt,
            kv_p[:, :kw].reshape(1, t, nh_f, dh),
            kv_p[:, kw:].reshape(1, t, nh_f, dh),
            logf[:t, :nh_f].reshape(1, t, nh_f),
            ret_state_sample,
            sample_rows(kv_s[:, :kw]).reshape(n_seq, n_tok, nh_f, dh),
            sample_rows(kv_s[:, kw:]).reshape(n_seq, n_tok, nh_f, dh),
            sample_rows(logf[t:, :nh_f]))
```

```python
import functools
import math

import jax
import jax.numpy as jnp
from jax import lax
from jax.experimental import pallas as pl
from jax.experimental.pallas import tpu as pltpu

F32 = jnp.float32
BF16 = jnp.bfloat16

EPS = 1e-6
ROPE_BASE = 10000.0
RET_HEADS = 8
RET_CHUNK = 128
FOX_HEADS = 16
PAGE_SIZE = 128
LANES = 128
SAMPLE_ROWS = 8
VMEM_LIMIT = 56 * 1024 * 1024


def _cparams(*semantics):
    return pltpu.CompilerParams(dimension_semantics=semantics, vmem_limit_bytes=VMEM_LIMIT)


def _dot(a, b):
    return jnp.dot(a, b, preferred_element_type=F32)


def _dot_nt(a, b):
    return lax.dot_general(a, b, (((1,), (1,)), ((), ())), preferred_element_type=F32)


def _dot_tn(a, b):
    return lax.dot_general(a, b, (((0,), (0,)), ((), ())), preferred_element_type=F32)


def _log2(n):
    assert n > 0 and n & (n - 1) == 0, n
    return n.bit_length() - 1


def _rms_unit(x):
    return x * lax.rsqrt(jnp.mean(x * x, axis=-1, keepdims=True) + EPS)


def _norm_proj_kernel(*refs, mode, rope_tiles, kscale_from):
    if mode == "rope":
        x_ref, g_ref, w_ref, cos_ref, sin_ref, o_ref, h_scr = refs
    elif mode == "logsig":
        x_ref, g_ref, w_ref, b_ref, o_ref, h_scr = refs
    elif mode == "dual":
        x_ref, g_ref, w_ref, o_ref, o16_ref, h_scr = refs
    else:
        x_ref, g_ref, w_ref, o_ref, h_scr = refs
    j = pl.program_id(1)

    @pl.when(j == 0)
    def _():
        h_scr[...] = (_rms_unit(x_ref[...]) * g_ref[...]).astype(BF16)

    acc = _dot(h_scr[...], w_ref[...])

    if mode == "rope":
        @pl.when(j < rope_tiles)
        def _():
            cos = cos_ref[...]
            sin = sin_ref[...]
            half = cos.shape[1]
            scale = jnp.where(j >= kscale_from, (2 * half) ** -0.5, 1.0).astype(F32)
            for hh in range(acc.shape[1] // (2 * half)):
                lo = hh * 2 * half
                x1 = acc[:, lo:lo + half]
                x2 = acc[:, lo + half:lo + 2 * half]
                o_ref[:, lo:lo + half] = (x1 * cos - x2 * sin) * scale
                o_ref[:, lo + half:lo + 2 * half] = (x1 * sin + x2 * cos) * scale

        @pl.when(j >= rope_tiles)
        def _():
            o_ref[...] = acc
    elif mode == "logsig":
        z = acc + b_ref[...]
        o_ref[...] = jnp.minimum(z, 0.0) - jnp.log1p(jnp.exp(-jnp.abs(z)))
    elif mode == "dual":
        o_ref[...] = acc
        o16_ref[...] = acc.astype(BF16)
    else:
        o_ref[...] = acc.astype(o_ref.dtype)


def _norm_proj(x, g, w, *, mode="plain", out_dtype=F32, tm, tn, row_blocks=None,
               cos=None, sin=None, bias=None, rope_tiles=0, kscale_from=0):
    m, k = x.shape
    n = w.shape[1]
    r0, nr = row_blocks if row_blocks is not None else (0, m // tm)
    tn = min(tn, n)
    assert m % tm == 0 and n % tn == 0
    grid = (nr, n // tn)
    row = lambda i, j: (i + r0, 0)
    in_specs = [pl.BlockSpec((tm, k), row),
                pl.BlockSpec((1, k), lambda i, j: (0, 0)),
                pl.BlockSpec((k, tn), lambda i, j: (0, j))]
    args = [x, g.reshape(1, k), w]
    if mode == "rope":
        in_specs += [pl.BlockSpec((tm, cos.shape[1]), row)] * 2
        args += [cos, sin]
    if mode == "logsig":
        in_specs += [pl.BlockSpec((1, tn), lambda i, j: (0, j))]
        args += [bias]
    out_spec = pl.BlockSpec((tm, tn), lambda i, j: (i, j))
    out_shape = jax.ShapeDtypeStruct((nr * tm, n), out_dtype)
    if mode == "dual":
        out_shape = (out_shape, jax.ShapeDtypeStruct((nr * tm, n), BF16))
        out_spec = (out_spec, out_spec)
    return pl.pallas_call(
        functools.partial(_norm_proj_kernel, mode=mode, rope_tiles=rope_tiles,
                          kscale_from=kscale_from),
        out_shape=out_shape, grid=grid, in_specs=in_specs, out_specs=out_spec,
        scratch_shapes=[pltpu.VMEM((tm, k), BF16)],
        compiler_params=_cparams("parallel", "arbitrary"),
        name="norm_proj_" + mode,
    )(*args)


def _proj_res_kernel(a_ref, w_ref, r_ref, o_ref):
    o_ref[...] = r_ref[...] + _dot(a_ref[...], w_ref[...])


def _proj_res(a, w, res, *, tm, tn):
    m, k = a.shape
    n = w.shape[1]
    tn = min(tn, n)
    assert m % tm == 0 and n % tn == 0
    return pl.pallas_call(
        _proj_res_kernel,
        out_shape=jax.ShapeDtypeStruct((m, n), F32),
        grid=(m // tm, n // tn),
        in_specs=[pl.BlockSpec((tm, k), lambda i, j: (i, 0)),
                  pl.BlockSpec((k, tn), lambda i, j: (0, j)),
                  pl.BlockSpec((tm, tn), lambda i, j: (i, j))],
        out_specs=pl.BlockSpec((tm, tn), lambda i, j: (i, j)),
        compiler_params=_cparams("parallel", "arbitrary"),
        name="proj_res",
    )(a, w, res)


def _mlp_kernel(*refs, final_norm):
    if final_norm:
        x_ref, g_ref, wu_ref, wd_ref, gf_ref, o_ref, h_scr, acc_scr = refs
    else:
        x_ref, g_ref, wu_ref, wd_ref, o_ref, h_scr, acc_scr = refs
    f = pl.program_id(1)

    @pl.when(f == 0)
    def _():
        h_scr[...] = (_rms_unit(x_ref[...]) * g_ref[...]).astype(BF16)
        acc_scr[...] = jnp.zeros_like(acc_scr)

    a = jnp.maximum(_dot(h_scr[...], wu_ref[...]), 0.0)
    acc_scr[...] += _dot((a * a).astype(BF16), wd_ref[...])

    @pl.when(f == pl.num_programs(1) - 1)
    def _():
        y = x_ref[...] + acc_scr[...]
        if final_norm:
            y = _rms_unit(y) * gf_ref[...]
        o_ref[...] = y


def _mlp(x, g, w_up, w_down, *, tm, tf, row_blocks=None, final_g=None):
    m, d = x.shape
    ff = w_up.shape[1]
    r0, nr = row_blocks if row_blocks is not None else (0, m // tm)
    assert m % tm == 0 and ff % tf == 0
    row = lambda i, f: (i + r0, 0)
    const = lambda i, f: (0, 0)
    in_specs = [pl.BlockSpec((tm, d), row),
                pl.BlockSpec((1, d), const),
                pl.BlockSpec((d, tf), lambda i, f: (0, f)),
                pl.BlockSpec((tf, d), lambda i, f: (f, 0))]
    args = [x, g.reshape(1, d), w_up, w_down]
    if final_g is not None:
        in_specs.append(pl.BlockSpec((1, d), const))
        args.append(final_g.reshape(1, d))
    return pl.pallas_call(
        functools.partial(_mlp_kernel, final_norm=final_g is not None),
        out_shape=jax.ShapeDtypeStruct((nr * tm, d), F32),
        grid=(nr, ff // tf),
        in_specs=in_specs,
        out_specs=pl.BlockSpec((tm, d), lambda i, f: (i, 0)),
        scratch_shapes=[pltpu.VMEM((tm, d), BF16), pltpu.VMEM((tm, d), F32)],
        compiler_params=_cparams("parallel", "arbitrary"),
        name="mlp",
    )(*args)


def _head_out(o, gate, ng):
    o = _rms_unit(o) * ng
    return gate / (1.0 + jnp.exp(-gate)) * o


def _ret_prompt_kernel(lg_ref, q_ref, k_ref, v_ref, g_ref, ng_ref, y_ref, s_ref, s_scr):
    h = pl.program_id(0)
    c = pl.program_id(1)
    chunk = q_ref.shape[0]
    lg = lg_ref[h]

    @pl.when(c == 0)
    def _():
        s_scr[...] = jnp.zeros_like(s_scr)

    q = q_ref[...].astype(BF16)
    k = k_ref[...]
    v = v_ref[...].astype(BF16)
    state = s_scr[...]

    ii = lax.broadcasted_iota(jnp.int32, (chunk, chunk), 0)
    jj = lax.broadcasted_iota(jnp.int32, (chunk, chunk), 1)
    diff = (ii - jj).astype(F32)
    decay = jnp.where(diff >= 0, jnp.exp(lg * jnp.maximum(diff, 0.0)), 0.0)
    idx = lax.broadcasted_iota(jnp.int32, (chunk, 1), 0).astype(F32)

    scores = _dot_nt(q, k.astype(BF16)) * decay
    inner = _dot(scores.astype(BF16), v)
    cross = _dot(q, state.astype(BF16)) * jnp.exp(lg * (idx + 1.0))
    k_scaled = (k * jnp.exp(lg * (chunk - 1.0 - idx))).astype(BF16)
    gamma_c = jnp.exp(lg * jnp.full((1, state.shape[1]), float(chunk), F32))
    s_scr[...] = gamma_c * state + _dot_tn(k_scaled, v)

    y_ref[...] = _head_out(inner + cross, g_ref[...], ng_ref[...]).astype(y_ref.dtype)

    @pl.when(c == pl.num_programs(1) - 1)
    def _():
        s_ref[0, 0, 0] = s_scr[...]


def _ret_prompt(proj, log_gamma, norm_g, *, t, dk, dv):
    nh = RET_HEADS
    chunk = RET_CHUNK if t % RET_CHUNK == 0 else t
    kb = nh * dk // dk
    vb = 2 * nh * dk // dv
    gb = vb + nh
    return pl.pallas_call(
        _ret_prompt_kernel,
        out_shape=(jax.ShapeDtypeStruct((t, nh * dv), BF16),
                   jax.ShapeDtypeStruct((1, 1, nh, dk, dv), F32)),
        grid_spec=pltpu.PrefetchScalarGridSpec(
            num_scalar_prefetch=0,
            grid=(nh, t // chunk),
            in_specs=[pl.BlockSpec(memory_space=pltpu.SMEM),
                      pl.BlockSpec((chunk, dk), lambda h, c: (c, h)),
                      pl.BlockSpec((chunk, dk), lambda h, c: (c, kb + h)),
                      pl.BlockSpec((chunk, dv), lambda h, c: (c, vb + h)),
                      pl.BlockSpec((chunk, dv), lambda h, c: (c, gb + h)),
                      pl.BlockSpec((1, dv), lambda h, c: (0, h))],
            out_specs=(pl.BlockSpec((chunk, dv), lambda h, c: (c, h)),
                       pl.BlockSpec((1, 1, 1, dk, dv), lambda h, c: (0, 0, h, 0, 0))),
            scratch_shapes=[pltpu.VMEM((dk, dv), F32)]),
        compiler_params=_cparams("parallel", "arbitrary"),
        name="ret_prompt",
    )(log_gamma, proj, proj, proj, proj, norm_g.reshape(1, nh * dv))


def _ret_sample_kernel(lg_ref, q_ref, k_ref, v_ref, g_ref, ng_ref, s0_ref, y_ref, s1_ref,
                       kpad_scr, vpad_scr, *, n_tok, dk, dv):
    rows = q_ref.shape[0]
    kpad_scr[...] = jnp.zeros_like(kpad_scr)
    vpad_scr[...] = jnp.zeros_like(vpad_scr)
    idx = lax.broadcasted_iota(jnp.int32, (rows, 1), 0).astype(F32)
    for h in range(RET_HEADS):
        lg = lg_ref[h]
        q = q_ref[:, h * dk:(h + 1) * dk]
        k = k_ref[:, h * dk:(h + 1) * dk]
        v = v_ref[:, h * dv:(h + 1) * dv]
        state = s0_ref[0, 0, h]

        cross = _dot(q.astype(BF16), state.astype(BF16)) * jnp.exp(lg * (idx + 1.0))
        inner = jnp.zeros((rows, dv), F32)
        for j in range(n_tok):
            s_j = jnp.sum(q * k[j:j + 1, :], axis=1, keepdims=True)
            d_j = jnp.where(idx >= j, jnp.exp(lg * jnp.maximum(idx - j, 0.0)), 0.0)
            inner = inner + (s_j * d_j) * v[j:j + 1, :]

        kpad_scr[0:rows, :] = k * jnp.exp(lg * (n_tok - 1.0 - idx))
        vpad_scr[0:rows, :] = v
        upd = _dot_tn(kpad_scr[...].astype(BF16), vpad_scr[...].astype(BF16))
        gamma_c = jnp.exp(lg * jnp.full((1, dv), float(n_tok), F32))
        s1_ref[0, 0, h] = gamma_c * state + upd

        y_ref[:, h * dv:(h + 1) * dv] = _head_out(
            inner + cross, g_ref[:, h * dv:(h + 1) * dv], ng_ref[:, h * dv:(h + 1) * dv])


def _ret_sample(proj, state, log_gamma, norm_g, *, row0, n_seq, n_tok, dk, dv):
    nh = RET_HEADS
    rows = SAMPLE_ROWS
    rb = row0 // rows
    qw = nh * dk
    vw = nh * dv
    return pl.pallas_call(
        functools.partial(_ret_sample_kernel, n_tok=n_tok, dk=dk, dv=dv),
        out_shape=(jax.ShapeDtypeStruct((n_seq * rows, vw), F32),
                   jax.ShapeDtypeStruct(state.shape, F32)),
        grid_spec=pltpu.PrefetchScalarGridSpec(
            num_scalar_prefetch=0,
            grid=(n_seq,),
            in_specs=[pl.BlockSpec(memory_space=pltpu.SMEM),
                      pl.BlockSpec((rows, qw), lambda b: (rb + b, 0)),
                      pl.BlockSpec((rows, qw), lambda b: (rb + b, 1)),
                      pl.BlockSpec((rows, vw), lambda b: (rb + b, 2 * qw // vw)),
                      pl.BlockSpec((rows, vw), lambda b: (rb + b, 2 * qw // vw + 1)),
                      pl.BlockSpec((1, vw), lambda b: (0, 0)),
                      pl.BlockSpec((1, 1, nh, dk, dv), lambda b: (0, b, 0, 0, 0))],
            out_specs=(pl.BlockSpec((rows, vw), lambda b: (b, 0)),
                       pl.BlockSpec((1, 1, nh, dk, dv), lambda b: (0, b, 0, 0, 0))),
            scratch_shapes=[pltpu.VMEM((LANES, dk), F32), pltpu.VMEM((LANES, dv), F32)]),
        compiler_params=_cparams("parallel"),
        name="ret_sample",
    )(log_gamma, proj, proj, proj, proj, norm_g.reshape(1, vw), state)


def _split3(x):
    hi = x.astype(BF16)
    r1 = x - hi.astype(F32)
    mid = r1.astype(BF16)
    lo = (r1 - mid.astype(F32)).astype(BF16)
    return hi, mid, lo


def _cum_bias_kernel(x_ref, qa_ref, ka_ref, carry_scr, *, inv_scale):
    @pl.when(pl.program_id(0) == 0)
    def _():
        carry_scr[...] = jnp.zeros_like(carry_scr)

    x = x_ref[...]
    n = x.shape[0]
    tri = (lax.broadcasted_iota(jnp.int32, (n, n), 0)
           >= lax.broadcasted_iota(jnp.int32, (n, n), 1)).astype(BF16)
    hi, mid, lo = _split3(x)
    c = _dot(tri, hi) + _dot(tri, mid) + _dot(tri, lo) + carry_scr[...]
    carry_scr[...] = c[n - 1:n, :]

    lane = lax.broadcasted_iota(jnp.int32, (n, LANES), 1)
    for h in range(qa_ref.shape[0]):
        b = jnp.broadcast_to(c[:, h:h + 1] * inv_scale, (n, LANES))
        hi, mid, lo = (term.astype(F32) for term in _split3(b))
        qa = jnp.where(lane == 0, hi, jnp.where(lane == 1, mid, jnp.where(
            lane == 2, lo, jnp.where(lane < 6, 1.0, 0.0))))
        ka = jnp.where(lane < 3, 1.0, jnp.where(lane == 3, -hi, jnp.where(
            lane == 4, -mid, jnp.where(lane == 5, -lo, 0.0))))
        qa_ref[h] = qa.astype(BF16)
        ka_ref[h] = ka.astype(BF16)


def _cum_bias(logf, *, rows, tb, scale):
    nh = FOX_HEADS
    out = jax.ShapeDtypeStruct((nh, rows, LANES), BF16)
    spec = pl.BlockSpec((nh, tb, LANES), lambda i: (0, i, 0))
    return pl.pallas_call(
        functools.partial(_cum_bias_kernel, inv_scale=1.0 / scale),
        out_shape=(out, out),
        grid=(rows // tb,),
        in_specs=[pl.BlockSpec((tb, logf.shape[1]), lambda i: (i, 0))],
        out_specs=(spec, spec),
        scratch_shapes=[pltpu.VMEM((1, logf.shape[1]), F32)],
        compiler_params=_cparams("arbitrary"),
        name="cum_bias",
    )(logf)


def _flash_kernel(q_ref, qa_ref, k_ref, ka_ref, v_ref, o_ref, *, scale, tk, n_chain):
    i = pl.program_id(1)
    tq, dh = q_ref.shape
    assert tq % tk == 0
    c = scale * math.log2(math.e)
    q = jnp.concatenate([q_ref[...].astype(BF16), qa_ref[0]], axis=1)

    ts = tq // n_chain
    q_rows = [q[g * ts:(g + 1) * ts] for g in range(n_chain)]

    def step(j, carries, diag_offset=None):
        start = pl.multiple_of(j * tk, tk)
        k = jnp.concatenate([k_ref[pl.ds(start, tk), :], ka_ref[0, pl.ds(start, tk), :]], axis=1)
        v = v_ref[pl.ds(start, tk), :]
        out = []
        for g, (m, l, acc) in enumerate(carries):
            s = _dot_nt(q_rows[g], k)
            if diag_offset is not None:
                qpos = lax.broadcasted_iota(jnp.int32, (ts, tk), 0) + g * ts
                kpos = lax.broadcasted_iota(jnp.int32, (ts, tk), 1) + diag_offset
                s = jnp.where(kpos <= qpos, s, -jnp.inf)
            m_new = jnp.maximum(m, jnp.max(s, axis=1, keepdims=True))
            p = jnp.exp2(c * s - c * m_new)
            alpha = jnp.exp2(c * (m - m_new))
            l = alpha * l + jnp.sum(p, axis=1, keepdims=True)
            acc = alpha * acc + _dot(p.astype(BF16), v)
            out.append((m_new, l, acc))
        return tuple(out)

    carries = tuple((jnp.full((ts, 1), -jnp.inf, F32), jnp.zeros((ts, 1), F32),
                     jnp.zeros((ts, dh), F32)) for _ in range(n_chain))
    n_diag = tq // tk
    carries = lax.fori_loop(0, i * n_diag, lambda j, cr: step(j, cr), carries)
    for d in range(n_diag):
        carries = step(i * n_diag + d, carries, diag_offset=d * tk)
    for g, (m, l, acc) in enumerate(carries):
        o_ref[g * ts:(g + 1) * ts, :] = (acc / l).astype(o_ref.dtype)


def _fox_prompt(q, kv16, q_extra, k_extra, *, t, dh, tq, tk, n_chain):
    nh = FOX_HEADS
    return pl.pallas_call(
        functools.partial(_flash_kernel, scale=dh ** -0.5, tk=tk, n_chain=n_chain),
        out_shape=jax.ShapeDtypeStruct((t, nh * dh), BF16),
        grid=(nh, t // tq),
        in_specs=[pl.BlockSpec((tq, dh), lambda h, i: (i, h)),
                  pl.BlockSpec((1, tq, LANES), lambda h, i: (h, i, 0)),
                  pl.BlockSpec((t, dh), lambda h, i: (0, h)),
                  pl.BlockSpec((1, t, LANES), lambda h, i: (h, 0, 0)),
                  pl.BlockSpec((t, dh), lambda h, i: (0, nh + h))],
        out_specs=pl.BlockSpec((tq, dh), lambda h, i: (i, h)),
        compiler_params=_cparams("parallel", "arbitrary"),
        name="fox_prompt",
    )(q, q_extra, kv16, k_extra, kv16)


def _head_suffix_scan(x, nh):
    n = x.shape[1]
    lane = lax.broadcasted_iota(jnp.int32, x.shape, 1)
    y = x
    s = nh
    while s < n:
        shifted = pltpu.roll(y, n - s, axis=1)
        y = y + jnp.where(lane + s < n, shifted, 0.0)
        s *= 2
    return y


def _paged_kernel(pt_ref, q_ref, kn_ref, vn_ref, lfn_ref, *refs, n_tok, pages_per_step, scale):
    del pt_ref
    pps = pages_per_step
    k_refs = refs[0:pps]
    v_refs = refs[pps:2 * pps]
    lf_refs = refs[2 * pps:3 * pps]
    o_ref, mask_scr, m_scr, l_scr, acc_scr, carry_scr = refs[3 * pps:]
    nh = FOX_HEADS
    nq = n_tok * nh
    cols = mask_scr.shape[1]
    step = pl.program_id(1)

    r1 = lax.broadcasted_iota(jnp.int32, (nq, LANES), 0)
    c1 = lax.broadcasted_iota(jnp.int32, (nq, LANES), 1)
    onehot = (c1 == jnp.bitwise_and(r1, nh - 1)).astype(F32)

    @pl.when(step == 0)
    def _():
        row = lax.broadcasted_iota(jnp.int32, (nq, cols), 0)
        col = lax.broadcasted_iota(jnp.int32, (nq, cols), 1)
        same_head = jnp.bitwise_and(row, nh - 1) == jnp.bitwise_and(col, nh - 1)
        mask_scr[...] = jnp.where(same_head, 0.0, -jnp.inf)

        q = q_ref[0]
        kn = kn_ref[0]
        vn = vn_ref[0]
        lfn = lfn_ref[...]
        t_of_row = lax.shift_right_logical(lax.broadcasted_iota(jnp.int32, (nq, 1), 0), _log2(nh))
        cn = jnp.zeros((1, LANES), F32)
        s_new = []
        for t in range(n_tok):
            cn = cn + lfn[t:t + 1, :]
            bias = jnp.sum(onehot * cn, axis=1, keepdims=True)
            k_t = jnp.concatenate([kn[t * nh:(t + 1) * nh, :]] * n_tok, axis=0)
            s_t = jnp.sum(q * k_t, axis=1, keepdims=True) * scale - bias
            s_new.append(jnp.where(t_of_row >= t, s_t, -jnp.inf))
        m0 = s_new[0]
        for t in range(1, n_tok):
            m0 = jnp.maximum(m0, s_new[t])
        l0 = jnp.zeros((nq, 1), F32)
        acc0 = jnp.zeros(acc_scr.shape, F32)
        for t in range(n_tok):
            p_t = jnp.exp(s_new[t] - m0)
            l0 = l0 + p_t
            acc0 = acc0 + p_t * jnp.concatenate([vn[t * nh:(t + 1) * nh, :]] * n_tok, axis=0)
        m_scr[...] = m0
        l_scr[...] = l0
        acc_scr[...] = acc0
        carry_scr[...] = jnp.zeros_like(carry_scr)

    q16 = q_ref[0].astype(BF16)
    carry = carry_scr[...]
    m_prev = m_scr[...]
    m_new = m_prev
    s_pages = []
    shifts = []
    for r in range(pps):
        lf = lf_refs[r][0]
        incl = _head_suffix_scan(lf, nh)
        bias = mask_scr[...] + (incl - lf)
        s = _dot_nt(q16, k_refs[r][0].astype(BF16)) * scale + bias
        m_new = jnp.maximum(m_new, jnp.max(s, axis=1, keepdims=True) + carry)
        s_pages.append(s)
        shifts.append(carry)
        carry = carry + jnp.sum(onehot * incl[:, 0:LANES], axis=1, keepdims=True)
    carry_scr[...] = carry

    alpha = jnp.exp(m_prev - m_new)
    l_new = alpha * l_scr[...]
    acc = alpha * acc_scr[...]
    for r in range(pps):
        p = jnp.exp(s_pages[r] - (m_new - shifts[r]))
        l_new = l_new + jnp.sum(p, axis=1, keepdims=True)
        acc = acc + _dot(p.astype(BF16), v_refs[r][0].astype(BF16))
    m_scr[...] = m_new
    l_scr[...] = l_new
    acc_scr[...] = acc

    @pl.when(step == pl.num_programs(1) - 1)
    def _():
        o_ref[0] = acc / l_new


def _fox_paged(q, k_new, v_new, logf_new, cache_k, cache_v, cache_lf, page_table, *,
               n_tok, pages_per_step):
    nh = FOX_HEADS
    n_seq, n_pages = page_table.shape
    _, nq, dh = q.shape
    cols = PAGE_SIZE * nh
    pps = pages_per_step
    assert n_pages % pps == 0 and nq == n_tok * nh

    def tok(b, s, pt):
        return (b, 0, 0)

    def page(r):
        return lambda b, s, pt: (pt[b, n_pages - 1 - (s * pps + r)], 0, 0)

    in_specs = [pl.BlockSpec((1, nq, dh), tok)] * 3
    in_specs += [pl.BlockSpec((SAMPLE_ROWS, LANES), lambda b, s, pt: (b, 0))]
    in_specs += [pl.BlockSpec((1, cols, dh), page(r)) for r in range(pps)] * 2
    in_specs += [pl.BlockSpec((1, 1, cols), page(r)) for r in range(pps)]
    args = [q, k_new, v_new, logf_new] + [cache_k] * pps + [cache_v] * pps + [cache_lf] * pps
    return pl.pallas_call(
        functools.partial(_paged_kernel, n_tok=n_tok, pages_per_step=pps, scale=dh ** -0.5),
        out_shape=jax.ShapeDtypeStruct((n_seq, nq, dh), F32),
        grid_spec=pltpu.PrefetchScalarGridSpec(
            num_scalar_prefetch=1,
            grid=(n_seq, n_pages // pps),
            in_specs=in_specs,
            out_specs=pl.BlockSpec((1, nq, dh), tok),
            scratch_shapes=[pltpu.VMEM((nq, cols), F32),
                            pltpu.VMEM((nq, 1), F32),
                            pltpu.VMEM((nq, 1), F32),
                            pltpu.VMEM((nq, dh), F32),
                            pltpu.VMEM((nq, 1), F32)]),
        compiler_params=_cparams("parallel", "arbitrary"),
        name="fox_paged",
    )(page_table, *args)


def kernel(x_prompt, x_sample, state_ret, cache_k, cache_v, cache_logf, page_table, norm_a_g,
           w_ret_in, ret_norm_g, w_ret_o, norm_kv_g, w_kvf, b_f, norm_b_g, w_fox_q, w_fox_o,
           norm_mlp_g, w_mlp_up, w_mlp_down, norm_f_g):
    n_b, t, d = x_prompt.shape
    n_seq, n_tok, _ = x_sample.shape
    assert n_b == 1 and n_tok <= SAMPLE_ROWS
    n_a = state_ret.shape[0]
    n_layers = w_mlp_up.shape[0]
    assert n_a == 1 and n_layers == 2
    dk, dv = state_ret.shape[3], state_ret.shape[4]
    nh_r = RET_HEADS
    nh_f = FOX_HEADS
    dh = d // nh_f
    past = page_table.shape[1] * PAGE_SIZE
    ms = n_seq * SAMPLE_ROWS
    tm = 512
    assert t % tm == 0 and ms % tm == 0
    p_blocks = (0, t // tm)
    s_blocks = (t // tm, ms // tm)

    xs = jnp.pad(x_sample, ((0, 0), (0, SAMPLE_ROWS - n_tok), (0, 0))).reshape(ms, d)
    x = jnp.concatenate([x_prompt.reshape(t, d), xs], axis=0)

    half = dk // 2
    pos = jnp.concatenate([jnp.arange(t, dtype=jnp.int32),
                           jnp.tile(past + jnp.arange(SAMPLE_ROWS, dtype=jnp.int32), n_seq)])
    inv = 1.0 / (ROPE_BASE ** (jnp.arange(half, dtype=F32) / half))
    ang = pos.astype(F32)[:, None] * inv[None, :]
    cos = jnp.cos(ang)
    sin = jnp.sin(ang)
    log_gamma = jnp.log1p(-(2.0 ** (-5.0 - jnp.arange(nh_r, dtype=F32))))

    tn = 512
    qk_w = 2 * nh_r * dk
    proj = _norm_proj(x, norm_a_g[0], w_ret_in[0].astype(BF16), mode="rope", tm=tm, tn=tn,
                      cos=cos, sin=sin, rope_tiles=qk_w // tn, kscale_from=qk_w // (2 * tn))
    y_p, ret_state_prompt = _ret_prompt(proj, log_gamma, ret_norm_g[0], t=t, dk=dk, dv=dv)
    y_s, ret_state_sample = _ret_sample(proj, state_ret, log_gamma, ret_norm_g[0], row0=t,
                                        n_seq=n_seq, n_tok=n_tok, dk=dk, dv=dv)
    y = jnp.concatenate([y_p, y_s.astype(BF16)], axis=0)
    x = _proj_res(y, w_ret_o[0].astype(BF16), x, tm=tm, tn=1024)
    x = _mlp(x, norm_mlp_g[0], w_mlp_up[0].astype(BF16), w_mlp_down[0].astype(BF16),
             tm=tm, tf=512)

    kv_w = 2 * nh_f * dh
    w_kv = w_kvf[:, :kv_w].astype(BF16)
    w_f = jnp.pad(w_kvf[:, kv_w:], ((0, 0), (0, LANES - nh_f))).astype(BF16)
    b_pad = jnp.pad(b_f, (0, LANES - nh_f)).reshape(1, LANES)
    kv_p, kv16_p = _norm_proj(x, norm_kv_g, w_kv, mode="dual", tm=tm, tn=1024,
                              row_blocks=p_blocks)
    kv_s = _norm_proj(x, norm_kv_g, w_kv, tm=tm, tn=1024, row_blocks=s_blocks)
    logf = _norm_proj(x, norm_kv_g, w_f, mode="logsig", bias=b_pad, tm=tm, tn=LANES)

    q = _norm_proj(x, norm_b_g[0], w_fox_q[0].astype(BF16), tm=tm, tn=1024)
    q_extra, k_extra = _cum_bias(logf, rows=t, tb=512, scale=dh ** -0.5)
    a_p = _fox_prompt(q, kv16_p, q_extra, k_extra, t=t, dh=dh,
                      tq=min(t, 1024), tk=min(t, 1024), n_chain=1)

    def head_rows(z):
        return z.reshape(n_seq, SAMPLE_ROWS, nh_f, dh)[:, :n_tok].reshape(n_seq, n_tok * nh_f, dh)

    n_phys = cache_k.shape[0]
    a_s = _fox_paged(head_rows(q[t:]), head_rows(kv_s[:, :nh_f * dh]),
                     head_rows(kv_s[:, nh_f * dh:]), logf[t:],
                     cache_k.reshape(n_phys, PAGE_SIZE * nh_f, dh),
                     cache_v.reshape(n_phys, PAGE_SIZE * nh_f, dh),
                     cache_logf.reshape(n_phys, 1, PAGE_SIZE * nh_f), page_table,
                     n_tok=n_tok, pages_per_step=min(4, page_table.shape[1]))
    a_s = jnp.pad(a_s.reshape(n_seq, n_tok, nh_f * dh).astype(BF16),
                  ((0, 0), (0, SAMPLE_ROWS - n_tok), (0, 0))).reshape(ms, nh_f * dh)
    a = jnp.concatenate([a_p, a_s], axis=0)
    x = _proj_res(a, w_fox_o[0].astype(BF16), x, tm=tm, tn=1024)
    wu1 = w_mlp_up[1].astype(BF16)
    wd1 = w_mlp_down[1].astype(BF16)
    y_prompt = _mlp(x, norm_mlp_g[1], wu1, wd1, tm=tm, tf=512, row_blocks=p_blocks,
                    final_g=norm_f_g)
    y_sample = _mlp(x, norm_mlp_g[1], wu1, wd1, tm=tm, tf=512, row_blocks=s_blocks,
                    final_g=norm_f_g)

    def sample_rows(z):
        return z.reshape(n_seq, SAMPLE_ROWS, -1)[:, :n_tok]

    kw = nh_f * dh
    return (y_prompt.reshape(1, t, d),
            sample_rows(y_sample),
            ret_state_prompt,
            kv_p[:, :kw].reshape(1, t, nh_f, dh),
            kv_p[:, kw:].reshape(1, t, nh_f, dh),
            logf[:t, :nh_f].reshape(1, t, nh_f),
            ret_state_sample,
            sample_rows(kv_s[:, :kw]).reshape(n_seq, n_tok, nh_f, dh),
            sample_rows(kv_s[:, kw:]).reshape(n_seq, n_tok, nh_f, dh),
            sample_rows(logf[t:, :nh_f]))
```

```python
import functools
import math

import jax
import jax.numpy as jnp
from jax import lax
from jax.experimental import pallas as pl
from jax.experimental.pallas import tpu as pltpu

F32 = jnp.float32
BF16 = jnp.bfloat16

EPS = 1e-6
ROPE_BASE = 10000.0
RET_HEADS = 8
RET_CHUNK = 128
FOX_HEADS = 16
PAGE_SIZE = 128
LANES = 128
SAMPLE_ROWS = 8
VMEM_LIMIT = 56 * 1024 * 1024


def _cparams(*semantics):
    return pltpu.CompilerParams(dimension_semantics=semantics, vmem_limit_bytes=VMEM_LIMIT)


def _dot(a, b):
    return jnp.dot(a, b, preferred_element_type=F32)


def _dot_nt(a, b):
    return lax.dot_general(a, b, (((1,), (1,)), ((), ())), preferred_element_type=F32)


def _dot_tn(a, b):
    return lax.dot_general(a, b, (((0,), (0,)), ((), ())), preferred_element_type=F32)


def _log2(n):
    assert n > 0 and n & (n - 1) == 0, n
    return n.bit_length() - 1


def _rms_unit(x):
    return x * lax.rsqrt(jnp.mean(x * x, axis=-1, keepdims=True) + EPS)


def _norm_proj_kernel(*refs, mode, rope_tiles, kscale_from):
    if mode == "rope":
        x_ref, g_ref, w_ref, cos_ref, sin_ref, o_ref, h_scr = refs
    elif mode == "logsig":
        x_ref, g_ref, w_ref, b_ref, o_ref, h_scr = refs
    elif mode == "dual":
        x_ref, g_ref, w_ref, o_ref, o16_ref, h_scr = refs
    else:
        x_ref, g_ref, w_ref, o_ref, h_scr = refs
    j = pl.program_id(1)

    @pl.when(j == 0)
    def _():
        h_scr[...] = (_rms_unit(x_ref[...]) * g_ref[...]).astype(BF16)

    acc = _dot(h_scr[...], w_ref[...])

    if mode == "rope":
        @pl.when(j < rope_tiles)
        def _():
            cos = cos_ref[...]
            sin = sin_ref[...]
            half = cos.shape[1]
            scale = jnp.where(j >= kscale_from, (2 * half) ** -0.5, 1.0).astype(F32)
            for hh in range(acc.shape[1] // (2 * half)):
                lo = hh * 2 * half
                x1 = acc[:, lo:lo + half]
                x2 = acc[:, lo + half:lo + 2 * half]
                o_ref[:, lo:lo + half] = (x1 * cos - x2 * sin) * scale
                o_ref[:, lo + half:lo + 2 * half] = (x1 * sin + x2 * cos) * scale

        @pl.when(j >= rope_tiles)
        def _():
            o_ref[...] = acc
    elif mode == "logsig":
        z = acc + b_ref[...]
        o_ref[...] = jnp.minimum(z, 0.0) - jnp.log1p(jnp.exp(-jnp.abs(z)))
    elif mode == "dual":
        o_ref[...] = acc
        o16_ref[...] = acc.astype(BF16)
    else:
        o_ref[...] = acc.astype(o_ref.dtype)


def _norm_proj(x, g, w, *, mode="plain", out_dtype=F32, tm, tn, row_blocks=None,
               cos=None, sin=None, bias=None, rope_tiles=0, kscale_from=0):
    m, k = x.shape
    n = w.shape[1]
    r0, nr = row_blocks if row_blocks is not None else (0, m // tm)
    tn = min(tn, n)
    assert m % tm == 0 and n % tn == 0
    grid = (nr, n // tn)
    row = lambda i, j: (i + r0, 0)
    in_specs = [pl.BlockSpec((tm, k), row),
                pl.BlockSpec((1, k), lambda i, j: (0, 0)),
                pl.BlockSpec((k, tn), lambda i, j: (0, j))]
    args = [x, g.reshape(1, k), w]
    if mode == "rope":
        in_specs += [pl.BlockSpec((tm, cos.shape[1]), row)] * 2
        args += [cos, sin]
    if mode == "logsig":
        in_specs += [pl.BlockSpec((1, tn), lambda i, j: (0, j))]
        args += [bias]
    out_spec = pl.BlockSpec((tm, tn), lambda i, j: (i, j))
    out_shape = jax.ShapeDtypeStruct((nr * tm, n), out_dtype)
    if mode == "dual":
        out_shape = (out_shape, jax.ShapeDtypeStruct((nr * tm, n), BF16))
        out_spec = (out_spec, out_spec)
    return pl.pallas_call(
        functools.partial(_norm_proj_kernel, mode=mode, rope_tiles=rope_tiles,
                          kscale_from=kscale_from),
        out_shape=out_shape, grid=grid, in_specs=in_specs, out_specs=out_spec,
        scratch_shapes=[pltpu.VMEM((tm, k), BF16)],
        compiler_params=_cparams("parallel", "arbitrary"),
        name="norm_proj_" + mode,
    )(*args)


def _proj_res_kernel(a_ref, w_ref, r_ref, o_ref):
    o_ref[...] = r_ref[...] + _dot(a_ref[...], w_ref[...])


def _proj_res(a, w, res, *, tm, tn):
    m, k = a.shape
    n = w.shape[1]
    tn = min(tn, n)
    assert m % tm == 0 and n % tn == 0
    return pl.pallas_call(
        _proj_res_kernel,
        out_shape=jax.ShapeDtypeStruct((m, n), F32),
        grid=(m // tm, n // tn),
        in_specs=[pl.BlockSpec((tm, k), lambda i, j: (i, 0)),
                  pl.BlockSpec((k, tn), lambda i, j: (0, j)),
                  pl.BlockSpec((tm, tn), lambda i, j: (i, j))],
        out_specs=pl.BlockSpec((tm, tn), lambda i, j: (i, j)),
        compiler_params=_cparams("parallel", "arbitrary"),
        name="proj_res",
    )(a, w, res)


def _mlp_kernel(*refs, final_norm):
    if final_norm:
        x_ref, g_ref, wu_ref, wd_ref, gf_ref, o_ref, h_scr = refs
    else:
        x_ref, g_ref, wu_ref, wd_ref, o_ref, h_scr = refs
    f = pl.program_id(1)

    @pl.when(f == 0)
    def _():
        x = x_ref[...]
        h_scr[...] = (_rms_unit(x) * g_ref[...]).astype(BF16)
        o_ref[...] = x

    a = jnp.maximum(_dot(h_scr[...], wu_ref[...]), 0.0)
    o_ref[...] += _dot((a * a).astype(BF16), wd_ref[...])

    if final_norm:
        @pl.when(f == pl.num_programs(1) - 1)
        def _():
            o_ref[...] = _rms_unit(o_ref[...]) * gf_ref[...]


def _mlp(x, g, w_up, w_down, *, layer, tm, tf, row_blocks=None, final_g=None):
    m, d = x.shape
    ff = w_up.shape[2]
    r0, nr = row_blocks if row_blocks is not None else (0, m // tm)
    assert m % tm == 0 and ff % tf == 0
    row = lambda i, f: (i + r0, 0)
    const = lambda i, f: (0, 0)
    in_specs = [pl.BlockSpec((tm, d), row),
                pl.BlockSpec((1, d), const),
                pl.BlockSpec((None, d, tf), lambda i, f: (layer, 0, f)),
                pl.BlockSpec((None, tf, d), lambda i, f: (layer, f, 0))]
    args = [x, g.reshape(1, d), w_up, w_down]
    if final_g is not None:
        in_specs.append(pl.BlockSpec((1, d), const))
        args.append(final_g.reshape(1, d))
    return pl.pallas_call(
        functools.partial(_mlp_kernel, final_norm=final_g is not None),
        out_shape=jax.ShapeDtypeStruct((nr * tm, d), F32),
        grid=(nr, ff // tf),
        in_specs=in_specs,
        out_specs=pl.BlockSpec((tm, d), lambda i, f: (i, 0)),
        scratch_shapes=[pltpu.VMEM((tm, d), BF16)],
        compiler_params=_cparams("parallel", "arbitrary"),
        name="mlp",
    )(*args)


def _head_out(o, gate, ng):
    o = _rms_unit(o) * ng
    return gate / (1.0 + jnp.exp(-gate)) * o


def _ret_prompt_kernel(lg_ref, q_ref, k_ref, v_ref, g_ref, ng_ref, y_ref, s_ref, s_scr, *,
                       dk, dv):
    hb = pl.program_id(0)
    c = pl.program_id(1)
    chunk = q_ref.shape[0]
    heads = s_scr.shape[0]

    @pl.when(c == 0)
    def _():
        s_scr[...] = jnp.zeros_like(s_scr)

    ii = lax.broadcasted_iota(jnp.int32, (chunk, chunk), 0)
    jj = lax.broadcasted_iota(jnp.int32, (chunk, chunk), 1)
    diff = (ii - jj).astype(F32)
    idx = lax.broadcasted_iota(jnp.int32, (chunk, 1), 0).astype(F32)

    for hh in range(heads):
        lg = lg_ref[hb * heads + hh]
        q = q_ref[:, hh * dk:(hh + 1) * dk].astype(BF16)
        k = k_ref[:, hh * dk:(hh + 1) * dk]
        v = v_ref[:, hh * dv:(hh + 1) * dv].astype(BF16)
        state = s_scr[hh]

        decay = jnp.where(diff >= 0, jnp.exp(lg * jnp.maximum(diff, 0.0)), 0.0)
        scores = _dot_nt(q, k.astype(BF16)) * decay
        inner = _dot(scores.astype(BF16), v)
        cross = _dot(q, state.astype(BF16)) * jnp.exp(lg * (idx + 1.0))
        k_scaled = (k * jnp.exp(lg * (chunk - 1.0 - idx))).astype(BF16)
        gamma_c = jnp.exp(lg * jnp.full((1, dv), float(chunk), F32))
        s_scr[hh] = gamma_c * state + _dot_tn(k_scaled, v)

        y_ref[:, hh * dv:(hh + 1) * dv] = _head_out(
            inner + cross, g_ref[:, hh * dv:(hh + 1) * dv],
            ng_ref[:, hh * dv:(hh + 1) * dv]).astype(y_ref.dtype)

    @pl.when(c == pl.num_programs(1) - 1)
    def _():
        s_ref[0, 0] = s_scr[...]


def _ret_prompt(proj, log_gamma, norm_g, *, t, dk, dv, heads_per_step):
    nh = RET_HEADS
    hps = heads_per_step
    assert nh % hps == 0
    nb = nh // hps
    chunk = RET_CHUNK if t % RET_CHUNK == 0 else t
    vb = 2 * nh * dk // (hps * dv)
    return pl.pallas_call(
        functools.partial(_ret_prompt_kernel, dk=dk, dv=dv),
        out_shape=(jax.ShapeDtypeStruct((t, nh * dv), BF16),
                   jax.ShapeDtypeStruct((1, 1, nh, dk, dv), F32)),
        grid_spec=pltpu.PrefetchScalarGridSpec(
            num_scalar_prefetch=0,
            grid=(nb, t // chunk),
            in_specs=[pl.BlockSpec(memory_space=pltpu.SMEM),
                      pl.BlockSpec((chunk, hps * dk), lambda h, c: (c, h)),
                      pl.BlockSpec((chunk, hps * dk), lambda h, c: (c, nb + h)),
                      pl.BlockSpec((chunk, hps * dv), lambda h, c: (c, vb + h)),
                      pl.BlockSpec((chunk, hps * dv), lambda h, c: (c, vb + nb + h)),
                      pl.BlockSpec((1, hps * dv), lambda h, c: (0, h))],
            out_specs=(pl.BlockSpec((chunk, hps * dv), lambda h, c: (c, h)),
                       pl.BlockSpec((1, 1, hps, dk, dv), lambda h, c: (0, 0, h, 0, 0))),
            scratch_shapes=[pltpu.VMEM((hps, dk, dv), F32)]),
        compiler_params=_cparams("parallel", "arbitrary"),
        name="ret_prompt",
    )(log_gamma, proj, proj, proj, proj, norm_g.reshape(1, nh * dv))


def _ret_sample_kernel(lg_ref, q_ref, k_ref, v_ref, g_ref, ng_ref, s0_ref, y_ref, s1_ref,
                       kpad_scr, vpad_scr, *, n_tok, dk, dv):
    rows = q_ref.shape[0]
    kpad_scr[...] = jnp.zeros_like(kpad_scr)
    vpad_scr[...] = jnp.zeros_like(vpad_scr)
    idx = lax.broadcasted_iota(jnp.int32, (rows, 1), 0).astype(F32)
    for h in range(RET_HEADS):
        lg = lg_ref[h]
        q = q_ref[:, h * dk:(h + 1) * dk]
        k = k_ref[:, h * dk:(h + 1) * dk]
        v = v_ref[:, h * dv:(h + 1) * dv]
        state = s0_ref[0, 0, h]

        cross = _dot(q.astype(BF16), state.astype(BF16)) * jnp.exp(lg * (idx + 1.0))
        inner = jnp.zeros((rows, dv), F32)
        for j in range(n_tok):
            s_j = jnp.sum(q * k[j:j + 1, :], axis=1, keepdims=True)
            d_j = jnp.where(idx >= j, jnp.exp(lg * jnp.maximum(idx - j, 0.0)), 0.0)
            inner = inner + (s_j * d_j) * v[j:j + 1, :]

        kpad_scr[0:rows, :] = k * jnp.exp(lg * (n_tok - 1.0 - idx))
        vpad_scr[0:rows, :] = v
        upd = _dot_tn(kpad_scr[...].astype(BF16), vpad_scr[...].astype(BF16))
        gamma_c = jnp.exp(lg * jnp.full((1, dv), float(n_tok), F32))
        s1_ref[0, 0, h] = gamma_c * state + upd

        y_ref[:, h * dv:(h + 1) * dv] = _head_out(
            inner + cross, g_ref[:, h * dv:(h + 1) * dv], ng_ref[:, h * dv:(h + 1) * dv])


def _ret_sample(proj, state, log_gamma, norm_g, *, row0, n_seq, n_tok, dk, dv):
    nh = RET_HEADS
    rows = SAMPLE_ROWS
    rb = row0 // rows
    qw = nh * dk
    vw = nh * dv
    return pl.pallas_call(
        functools.partial(_ret_sample_kernel, n_tok=n_tok, dk=dk, dv=dv),
        out_shape=(jax.ShapeDtypeStruct((n_seq * rows, vw), F32),
                   jax.ShapeDtypeStruct(state.shape, F32)),
        grid_spec=pltpu.PrefetchScalarGridSpec(
            num_scalar_prefetch=0,
            grid=(n_seq,),
            in_specs=[pl.BlockSpec(memory_space=pltpu.SMEM),
                      pl.BlockSpec((rows, qw), lambda b: (rb + b, 0)),
                      pl.BlockSpec((rows, qw), lambda b: (rb + b, 1)),
                      pl.BlockSpec((rows, vw), lambda b: (rb + b, 2 * qw // vw)),
                      pl.BlockSpec((rows, vw), lambda b: (rb + b, 2 * qw // vw + 1)),
                      pl.BlockSpec((1, vw), lambda b: (0, 0)),
                      pl.BlockSpec((1, 1, nh, dk, dv), lambda b: (0, b, 0, 0, 0))],
            out_specs=(pl.BlockSpec((rows, vw), lambda b: (b, 0)),
                       pl.BlockSpec((1, 1, nh, dk, dv), lambda b: (0, b, 0, 0, 0))),
            scratch_shapes=[pltpu.VMEM((LANES, dk), F32), pltpu.VMEM((LANES, dv), F32)]),
        compiler_params=_cparams("parallel"),
        name="ret_sample",
    )(log_gamma, proj, proj, proj, proj, norm_g.reshape(1, vw), state)


def _split3(x):
    hi = x.astype(BF16)
    r1 = x - hi.astype(F32)
    mid = r1.astype(BF16)
    lo = (r1 - mid.astype(F32)).astype(BF16)
    return hi, mid, lo


def _cum_bias_kernel(x_ref, qa_ref, ka_ref, carry_scr, *, inv_scale):
    @pl.when(pl.program_id(0) == 0)
    def _():
        carry_scr[...] = jnp.zeros_like(carry_scr)

    x = x_ref[...]
    n = x.shape[0]
    tri = (lax.broadcasted_iota(jnp.int32, (n, n), 0)
           >= lax.broadcasted_iota(jnp.int32, (n, n), 1)).astype(BF16)
    hi, mid, lo = _split3(x)
    c = _dot(tri, hi) + _dot(tri, mid) + _dot(tri, lo) + carry_scr[...]
    carry_scr[...] = c[n - 1:n, :]

    lane = lax.broadcasted_iota(jnp.int32, (n, LANES), 1)
    for h in range(qa_ref.shape[0]):
        b = jnp.broadcast_to(c[:, h:h + 1] * inv_scale, (n, LANES))
        hi, mid, lo = (term.astype(F32) for term in _split3(b))
        qa = jnp.where(lane == 0, hi, jnp.where(lane == 1, mid, jnp.where(
            lane == 2, lo, jnp.where(lane < 6, 1.0, 0.0))))
        ka = jnp.where(lane < 3, 1.0, jnp.where(lane == 3, -hi, jnp.where(
            lane == 4, -mid, jnp.where(lane == 5, -lo, 0.0))))
        qa_ref[h] = qa.astype(BF16)
        ka_ref[h] = ka.astype(BF16)


def _cum_bias(logf, *, rows, tb, scale):
    nh = FOX_HEADS
    out = jax.ShapeDtypeStruct((nh, rows, LANES), BF16)
    spec = pl.BlockSpec((nh, tb, LANES), lambda i: (0, i, 0))
    return pl.pallas_call(
        functools.partial(_cum_bias_kernel, inv_scale=1.0 / scale),
        out_shape=(out, out),
        grid=(rows // tb,),
        in_specs=[pl.BlockSpec((tb, logf.shape[1]), lambda i: (i, 0))],
        out_specs=(spec, spec),
        scratch_shapes=[pltpu.VMEM((1, logf.shape[1]), F32)],
        compiler_params=_cparams("arbitrary"),
        name="cum_bias",
    )(logf)


def _flash_kernel(q_ref, qa_ref, k_ref, ka_ref, v_ref, o_ref, *, scale, tk):
    i = pl.program_id(1)
    tq, dh = q_ref.shape
    assert tq % tk == 0
    c = scale * math.log2(math.e)
    q = jnp.concatenate([q_ref[...].astype(BF16), qa_ref[0]], axis=1)

    def step(j, carry, diag_offset=None):
        m, l, acc = carry
        start = pl.multiple_of(j * tk, tk)
        k = jnp.concatenate([k_ref[pl.ds(start, tk), :], ka_ref[0, pl.ds(start, tk), :]], axis=1)
        s = _dot_nt(q, k)
        if diag_offset is not None:
            qpos = lax.broadcasted_iota(jnp.int32, (tq, tk), 0)
            kpos = lax.broadcasted_iota(jnp.int32, (tq, tk), 1) + diag_offset
            s = jnp.where(kpos <= qpos, s, -jnp.inf)
        m_new = jnp.maximum(m, jnp.max(s, axis=1, keepdims=True))
        p = jnp.exp2(c * (s - m_new))
        alpha = jnp.exp2(c * (m - m_new))
        l = alpha * l + jnp.sum(p, axis=1, keepdims=True)
        acc = alpha * acc + _dot(p.astype(BF16), v_ref[pl.ds(start, tk), :])
        return m_new, l, acc

    carry = (jnp.full((tq, 1), -jnp.inf, F32), jnp.zeros((tq, 1), F32), jnp.zeros((tq, dh), F32))
    n_diag = tq // tk
    carry = lax.fori_loop(0, i * n_diag, lambda j, cr: step(j, cr), carry)
    for d in range(n_diag):
        carry = step(i * n_diag + d, carry, diag_offset=d * tk)
    m, l, acc = carry
    o_ref[...] = (acc / l).astype(o_ref.dtype)


def _fox_prompt(q, k16, v16, q_extra, k_extra, *, t, dh, tq, tk):
    nh = FOX_HEADS
    return pl.pallas_call(
        functools.partial(_flash_kernel, scale=dh ** -0.5, tk=tk),
        out_shape=jax.ShapeDtypeStruct((t, nh * dh), BF16),
        grid=(nh, t // tq),
        in_specs=[pl.BlockSpec((tq, dh), lambda h, i: (i, h)),
                  pl.BlockSpec((1, tq, LANES), lambda h, i: (h, i, 0)),
                  pl.BlockSpec((t, dh), lambda h, i: (0, h)),
                  pl.BlockSpec((1, t, LANES), lambda h, i: (h, 0, 0)),
                  pl.BlockSpec((t, dh), lambda h, i: (0, h))],
        out_specs=pl.BlockSpec((tq, dh), lambda h, i: (i, h)),
        compiler_params=_cparams("parallel", "arbitrary"),
        name="fox_prompt",
    )(q, q_extra, k16, k_extra, v16)


def _head_suffix_scan(x, nh):
    n = x.shape[1]
    lane = lax.broadcasted_iota(jnp.int32, x.shape, 1)
    y = x
    s = nh
    while s < n:
        shifted = pltpu.roll(y, n - s, axis=1)
        y = y + jnp.where(lane + s < n, shifted, 0.0)
        s *= 2
    return y


def _paged_kernel(pt_ref, q_ref, kn_ref, vn_ref, lfn_ref, *refs, n_tok, pages_per_step, scale):
    del pt_ref
    pps = pages_per_step
    k_refs = refs[0:pps]
    v_refs = refs[pps:2 * pps]
    lf_refs = refs[2 * pps:3 * pps]
    o_ref, mask_scr, m_scr, l_scr, acc_scr, carry_scr = refs[3 * pps:]
    nh = FOX_HEADS
    nq = n_tok * nh
    cols = mask_scr.shape[1]
    step = pl.program_id(1)

    r1 = lax.broadcasted_iota(jnp.int32, (nq, LANES), 0)
    c1 = lax.broadcasted_iota(jnp.int32, (nq, LANES), 1)
    onehot = (c1 == jnp.bitwise_and(r1, nh - 1)).astype(F32)

    @pl.when(step == 0)
    def _():
        row = lax.broadcasted_iota(jnp.int32, (nq, cols), 0)
        col = lax.broadcasted_iota(jnp.int32, (nq, cols), 1)
        same_head = jnp.bitwise_and(row, nh - 1) == jnp.bitwise_and(col, nh - 1)
        mask_scr[...] = jnp.where(same_head, 0.0, -jnp.inf)

        q = q_ref[0]
        kn = kn_ref[0]
        vn = vn_ref[0]
        lfn = lfn_ref[...]
        t_of_row = lax.shift_right_logical(lax.broadcasted_iota(jnp.int32, (nq, 1), 0), _log2(nh))
        cn = jnp.zeros((1, LANES), F32)
        s_new = []
        for t in range(n_tok):
            cn = cn + lfn[t:t + 1, :]
            bias = jnp.sum(onehot * cn, axis=1, keepdims=True)
            k_t = jnp.concatenate([kn[t * nh:(t + 1) * nh, :]] * n_tok, axis=0)
            s_t = jnp.sum(q * k_t, axis=1, keepdims=True) * scale - bias
            s_new.append(jnp.where(t_of_row >= t, s_t, -jnp.inf))
        m0 = s_new[0]
        for t in range(1, n_tok):
            m0 = jnp.maximum(m0, s_new[t])
        l0 = jnp.zeros((nq, 1), F32)
        acc0 = jnp.zeros(acc_scr.shape, F32)
        for t in range(n_tok):
            p_t = jnp.exp(s_new[t] - m0)
            l0 = l0 + p_t
            acc0 = acc0 + p_t * jnp.concatenate([vn[t * nh:(t + 1) * nh, :]] * n_tok, axis=0)
        m_scr[...] = m0
        l_scr[...] = l0
        acc_scr[...] = acc0
        carry_scr[...] = jnp.zeros_like(carry_scr)

    q16 = q_ref[0].astype(BF16)
    carry = carry_scr[...]
    m_prev = m_scr[...]
    m_new = m_prev
    s_pages = []
    shifts = []
    for r in range(pps):
        lf = lf_refs[r][0]
        incl = _head_suffix_scan(lf, nh)
        bias = mask_scr[...] + (incl - lf)
        s = _dot_nt(q16, k_refs[r][0].astype(BF16)) * scale + bias
        m_new = jnp.maximum(m_new, jnp.max(s, axis=1, keepdims=True) + carry)
        s_pages.append(s)
        shifts.append(carry)
        carry = carry + jnp.sum(onehot * incl[:, 0:LANES], axis=1, keepdims=True)
    carry_scr[...] = carry

    alpha = jnp.exp(m_prev - m_new)
    l_new = alpha * l_scr[...]
    acc = alpha * acc_scr[...]
    for r in range(pps):
        p = jnp.exp(s_pages[r] - (m_new - shifts[r]))
        l_new = l_new + jnp.sum(p, axis=1, keepdims=True)
        acc = acc + _dot(p.astype(BF16), v_refs[r][0].astype(BF16))
    m_scr[...] = m_new
    l_scr[...] = l_new
    acc_scr[...] = acc

    @pl.when(step == pl.num_programs(1) - 1)
    def _():
        o_ref[0] = acc / l_new


def _fox_paged(q, k_new, v_new, logf_new, cache_k, cache_v, cache_lf, page_table, *,
               n_tok, pages_per_step):
    nh = FOX_HEADS
    n_seq, n_pages = page_table.shape
    _, nq, dh = q.shape
    cols = PAGE_SIZE * nh
    pps = pages_per_step
    assert n_pages % pps == 0 and nq == n_tok * nh

    def tok(b, s, pt):
        return (b, 0, 0)

    def page(r):
        return lambda b, s, pt: (pt[b, n_pages - 1 - (s * pps + r)], 0, 0)

    in_specs = [pl.BlockSpec((1, nq, dh), tok)] * 3
    in_specs += [pl.BlockSpec((SAMPLE_ROWS, LANES), lambda b, s, pt: (b, 0))]
    in_specs += [pl.BlockSpec((1, cols, dh), page(r)) for r in range(pps)] * 2
    in_specs += [pl.BlockSpec((1, 1, cols), page(r)) for r in range(pps)]
    args = [q, k_new, v_new, logf_new] + [cache_k] * pps + [cache_v] * pps + [cache_lf] * pps
    return pl.pallas_call(
        functools.partial(_paged_kernel, n_tok=n_tok, pages_per_step=pps, scale=dh ** -0.5),
        out_shape=jax.ShapeDtypeStruct((n_seq, nq, dh), F32),
        grid_spec=pltpu.PrefetchScalarGridSpec(
            num_scalar_prefetch=1,
            grid=(n_seq, n_pages // pps),
            in_specs=in_specs,
            out_specs=pl.BlockSpec((1, nq, dh), tok),
            scratch_shapes=[pltpu.VMEM((nq, cols), F32),
                            pltpu.VMEM((nq, 1), F32),
                            pltpu.VMEM((nq, 1), F32),
                            pltpu.VMEM((nq, dh), F32),
                            pltpu.VMEM((nq, 1), F32)]),
        compiler_params=_cparams("parallel", "arbitrary"),
        name="fox_paged",
    )(page_table, *args)


def kernel(x_prompt, x_sample, state_ret, cache_k, cache_v, cache_logf, page_table, norm_a_g,
           w_ret_in, ret_norm_g, w_ret_o, norm_kv_g, w_kvf, b_f, norm_b_g, w_fox_q, w_fox_o,
           norm_mlp_g, w_mlp_up, w_mlp_down, norm_f_g):
    n_b, t, d = x_prompt.shape
    n_seq, n_tok, _ = x_sample.shape
    assert n_b == 1 and n_tok <= SAMPLE_ROWS
    n_a = state_ret.shape[0]
    n_layers = w_mlp_up.shape[0]
    assert n_a == 1 and n_layers == 2
    dk, dv = state_ret.shape[3], state_ret.shape[4]
    nh_r = RET_HEADS
    nh_f = FOX_HEADS
    dh = d // nh_f
    past = page_table.shape[1] * PAGE_SIZE
    ms = n_seq * SAMPLE_ROWS
    tm = 1024 if ms % 1024 == 0 else 512
    assert t % tm == 0 and ms % tm == 0
    p_blocks = (0, t // tm)
    s_blocks = (t // tm, ms // tm)

    xs = jnp.pad(x_sample, ((0, 0), (0, SAMPLE_ROWS - n_tok), (0, 0))).reshape(ms, d)
    x = jnp.concatenate([x_prompt.reshape(t, d), xs], axis=0)

    half = dk // 2
    pos = jnp.concatenate([jnp.arange(t, dtype=jnp.int32),
                           jnp.tile(past + jnp.arange(SAMPLE_ROWS, dtype=jnp.int32), n_seq)])
    inv = 1.0 / (ROPE_BASE ** (jnp.arange(half, dtype=F32) / half))
    ang = pos.astype(F32)[:, None] * inv[None, :]
    cos = jnp.cos(ang)
    sin = jnp.sin(ang)
    log_gamma = jnp.log1p(-(2.0 ** (-5.0 - jnp.arange(nh_r, dtype=F32))))

    tn = 512
    qk_w = 2 * nh_r * dk
    proj = _norm_proj(x, norm_a_g[0], w_ret_in[0].astype(BF16), mode="rope", tm=tm, tn=tn,
                      cos=cos, sin=sin, rope_tiles=qk_w // tn, kscale_from=qk_w // (2 * tn))
    y_p, ret_state_prompt = _ret_prompt(proj, log_gamma, ret_norm_g[0], t=t, dk=dk, dv=dv,
                                        heads_per_step=4)
    y_s, ret_state_sample = _ret_sample(proj, state_ret, log_gamma, ret_norm_g[0], row0=t,
                                        n_seq=n_seq, n_tok=n_tok, dk=dk, dv=dv)
    y = jnp.concatenate([y_p, y_s.astype(BF16)], axis=0)
    x = _proj_res(y, w_ret_o[0].astype(BF16), x, tm=tm, tn=512)
    w_up16 = w_mlp_up.astype(BF16)
    w_down16 = w_mlp_down.astype(BF16)
    x = _mlp(x, norm_mlp_g[0], w_up16, w_down16, layer=0, tm=tm, tf=512)

    kw = nh_f * dh
    w_kv = w_kvf[:, :2 * kw].astype(BF16)
    w_f = jnp.pad(w_kvf[:, 2 * kw:], ((0, 0), (0, LANES - nh_f))).astype(BF16)
    b_pad = jnp.pad(b_f, (0, LANES - nh_f)).reshape(1, LANES)
    k_p, k16_p = _norm_proj(x, norm_kv_g, w_kv[:, :kw], mode="dual", tm=tm, tn=1024,
                            row_blocks=p_blocks)
    v_p, v16_p = _norm_proj(x, norm_kv_g, w_kv[:, kw:], mode="dual", tm=tm, tn=1024,
                            row_blocks=p_blocks)
    kv_s = _norm_proj(x, norm_kv_g, w_kv, tm=tm, tn=1024, row_blocks=s_blocks)
    logf = _norm_proj(x, norm_kv_g, w_f, mode="logsig", bias=b_pad, tm=tm, tn=LANES)

    q = _norm_proj(x, norm_b_g[0], w_fox_q[0].astype(BF16), tm=tm, tn=1024)
    q_extra, k_extra = _cum_bias(logf, rows=t, tb=512, scale=dh ** -0.5)
    a_p = _fox_prompt(q, k16_p, v16_p, q_extra, k_extra, t=t, dh=dh,
                      tq=min(t, 1024), tk=min(t, 1024))

    def head_rows(z):
        return z.reshape(n_seq, SAMPLE_ROWS, nh_f, dh)[:, :n_tok].reshape(n_seq, n_tok * nh_f, dh)

    n_phys = cache_k.shape[0]
    a_s = _fox_paged(head_rows(q[t:]), head_rows(kv_s[:, :nh_f * dh]),
                     head_rows(kv_s[:, nh_f * dh:]), logf[t:],
                     cache_k.reshape(n_phys, PAGE_SIZE * nh_f, dh),
                     cache_v.reshape(n_phys, PAGE_SIZE * nh_f, dh),
                     cache_logf.reshape(n_phys, 1, PAGE_SIZE * nh_f), page_table,
                     n_tok=n_tok, pages_per_step=min(8, page_table.shape[1]))
    a_s = jnp.pad(a_s.reshape(n_seq, n_tok, nh_f * dh).astype(BF16),
                  ((0, 0), (0, SAMPLE_ROWS - n_tok), (0, 0))).reshape(ms, nh_f * dh)
    a = jnp.concatenate([a_p, a_s], axis=0)
    x = _proj_res(a, w_fox_o[0].astype(BF16), x, tm=tm, tn=1024)
    y_prompt = _mlp(x, norm_mlp_g[1], w_up16, w_down16, layer=1, tm=tm, tf=512,
                    row_blocks=p_blocks, final_g=norm_f_g)
    y_sample = _mlp(x, norm_mlp_g[1], w_up16, w_down16, layer=1, tm=tm, tf=512,
                    row_blocks=s_blocks, final_g=norm_f_g)

    def sample_rows(z):
        return z.reshape(n_seq, SAMPLE_ROWS, -1)[:, :n_tok]

    return (y_prompt.reshape(1, t, d),
            sample_rows(y_sample),
            ret_state_prompt,
            k_p.reshape(1, t, nh_f, dh),
            v_p.reshape(1, t, nh_f, dh),
            logf[:t, :nh_f].reshape(1, t, nh_f),
            ret_state_sample,
            sample_rows(kv_s[:, :kw]).reshape(n_seq, n_tok, nh_f, dh),
            sample_rows(kv_s[:, kw:]).reshape(n_seq, n_tok, nh_f, dh),
            sample_rows(logf[t:, :nh_f]))
```

```python
import functools
import math

import jax
import jax.numpy as jnp
from jax import lax
from jax.experimental import pallas as pl
from jax.experimental.pallas import tpu as pltpu

F32 = jnp.float32
BF16 = jnp.bfloat16

EPS = 1e-6
ROPE_BASE = 10000.0
RET_HEADS = 8
RET_CHUNK = 128
FOX_HEADS = 16
PAGE_SIZE = 128
LANES = 128
SAMPLE_ROWS = 8
VMEM_LIMIT = 56 * 1024 * 1024


def _cparams(*semantics):
    return pltpu.CompilerParams(dimension_semantics=semantics, vmem_limit_bytes=VMEM_LIMIT)


def _dot(a, b):
    return jnp.dot(a, b, preferred_element_type=F32)


def _dot_nt(a, b):
    return lax.dot_general(a, b, (((1,), (1,)), ((), ())), preferred_element_type=F32)


def _dot_tn(a, b):
    return lax.dot_general(a, b, (((0,), (0,)), ((), ())), preferred_element_type=F32)


def _log2(n):
    assert n > 0 and n & (n - 1) == 0, n
    return n.bit_length() - 1


def _rms_unit(x):
    return x * lax.rsqrt(jnp.mean(x * x, axis=-1, keepdims=True) + EPS)


def _norm_proj_kernel(*refs, mode, rope_tiles, kscale_from):
    if mode == "rope":
        x_ref, g_ref, w_ref, cos_ref, sin_ref, o_ref, h_scr = refs
    elif mode == "logsig":
        x_ref, g_ref, w_ref, b_ref, o_ref, h_scr = refs
    elif mode == "dual":
        x_ref, g_ref, w_ref, o_ref, o16_ref, h_scr = refs
    else:
        x_ref, g_ref, w_ref, o_ref, h_scr = refs
    j = pl.program_id(1)

    @pl.when(j == 0)
    def _():
        h_scr[...] = (_rms_unit(x_ref[...]) * g_ref[...]).astype(BF16)

    acc = _dot(h_scr[...], w_ref[...])

    if mode == "rope":
        @pl.when(j < rope_tiles)
        def _():
            cos = cos_ref[...]
            sin = sin_ref[...]
            half = cos.shape[1]
            scale = jnp.where(j >= kscale_from, (2 * half) ** -0.5, 1.0).astype(F32)
            for hh in range(acc.shape[1] // (2 * half)):
                lo = hh * 2 * half
                x1 = acc[:, lo:lo + half]
                x2 = acc[:, lo + half:lo + 2 * half]
                o_ref[:, lo:lo + half] = (x1 * cos - x2 * sin) * scale
                o_ref[:, lo + half:lo + 2 * half] = (x1 * sin + x2 * cos) * scale

        @pl.when(j >= rope_tiles)
        def _():
            o_ref[...] = acc
    elif mode == "logsig":
        z = acc + b_ref[...]
        o_ref[...] = jnp.minimum(z, 0.0) - jnp.log1p(jnp.exp(-jnp.abs(z)))
    elif mode == "dual":
        o_ref[...] = acc
        o16_ref[...] = acc.astype(BF16)
    else:
        o_ref[...] = acc.astype(o_ref.dtype)


def _norm_proj(x, g, w, *, mode="plain", out_dtype=F32, tm, tn, row_blocks=None,
               cos=None, sin=None, bias=None, rope_tiles=0, kscale_from=0):
    m, k = x.shape
    n = w.shape[1]
    r0, nr = row_blocks if row_blocks is not None else (0, m // tm)
    tn = min(tn, n)
    assert m % tm == 0 and n % tn == 0
    grid = (nr, n // tn)
    row = lambda i, j: (i + r0, 0)
    in_specs = [pl.BlockSpec((tm, k), row),
                pl.BlockSpec((1, k), lambda i, j: (0, 0)),
                pl.BlockSpec((k, tn), lambda i, j: (0, j))]
    args = [x, g.reshape(1, k), w]
    if mode == "rope":
        in_specs += [pl.BlockSpec((tm, cos.shape[1]), row)] * 2
        args += [cos, sin]
    if mode == "logsig":
        in_specs += [pl.BlockSpec((1, tn), lambda i, j: (0, j))]
        args += [bias]
    out_spec = pl.BlockSpec((tm, tn), lambda i, j: (i, j))
    out_shape = jax.ShapeDtypeStruct((nr * tm, n), out_dtype)
    if mode == "dual":
        out_shape = (out_shape, jax.ShapeDtypeStruct((nr * tm, n), BF16))
        out_spec = (out_spec, out_spec)
    return pl.pallas_call(
        functools.partial(_norm_proj_kernel, mode=mode, rope_tiles=rope_tiles,
                          kscale_from=kscale_from),
        out_shape=out_shape, grid=grid, in_specs=in_specs, out_specs=out_spec,
        scratch_shapes=[pltpu.VMEM((tm, k), BF16)],
        compiler_params=_cparams("parallel", "arbitrary"),
        name="norm_proj_" + mode,
    )(*args)


def _proj_res_kernel(ap_ref, as_ref, w_ref, r_ref, o_ref, *, n_prompt):
    i = pl.program_id(0)

    @pl.when(i < n_prompt)
    def _():
        o_ref[...] = r_ref[...] + _dot(ap_ref[...], w_ref[...])

    @pl.when(i >= n_prompt)
    def _():
        o_ref[...] = r_ref[...] + _dot(as_ref[...], w_ref[...])


def _proj_res(a_prompt, a_sample, w, res, *, tm, tn):
    t, k = a_prompt.shape
    ms = a_sample.shape[0]
    n = w.shape[1]
    tn = min(tn, n)
    assert t % tm == 0 and ms % tm == 0 and n % tn == 0 and res.shape[0] == t + ms
    n_p = t // tm
    return pl.pallas_call(
        functools.partial(_proj_res_kernel, n_prompt=n_p),
        out_shape=jax.ShapeDtypeStruct((t + ms, n), F32),
        grid=((t + ms) // tm, n // tn),
        in_specs=[pl.BlockSpec((tm, k), lambda i, j: (jnp.minimum(i, n_p - 1), 0)),
                  pl.BlockSpec((tm, k), lambda i, j: (jnp.maximum(i - n_p, 0), 0)),
                  pl.BlockSpec((k, tn), lambda i, j: (0, j)),
                  pl.BlockSpec((tm, tn), lambda i, j: (i, j))],
        out_specs=pl.BlockSpec((tm, tn), lambda i, j: (i, j)),
        compiler_params=_cparams("parallel", "arbitrary"),
        name="proj_res",
    )(a_prompt, a_sample, w, res)


def _mlp_kernel(*refs, final_norm):
    if final_norm:
        x_ref, g_ref, wu_ref, wd_ref, gf_ref, o_ref, h_scr = refs
    else:
        x_ref, g_ref, wu_ref, wd_ref, o_ref, h_scr = refs
    f = pl.program_id(1)

    @pl.when(f == 0)
    def _():
        x = x_ref[...]
        h_scr[...] = (_rms_unit(x) * g_ref[...]).astype(BF16)
        o_ref[...] = x

    a = jnp.maximum(_dot(h_scr[...], wu_ref[...]), 0.0)
    o_ref[...] += _dot((a * a).astype(BF16), wd_ref[...])

    if final_norm:
        @pl.when(f == pl.num_programs(1) - 1)
        def _():
            o_ref[...] = _rms_unit(o_ref[...]) * gf_ref[...]


def _mlp(x, g, w_up, w_down, *, layer, tm, tf, row_blocks=None, final_g=None):
    m, d = x.shape
    ff = w_up.shape[2]
    r0, nr = row_blocks if row_blocks is not None else (0, m // tm)
    assert m % tm == 0 and ff % tf == 0
    row = lambda i, f: (i + r0, 0)
    const = lambda i, f: (0, 0)
    in_specs = [pl.BlockSpec((tm, d), row),
                pl.BlockSpec((1, d), const),
                pl.BlockSpec((None, d, tf), lambda i, f: (layer, 0, f)),
                pl.BlockSpec((None, tf, d), lambda i, f: (layer, f, 0))]
    args = [x, g.reshape(1, d), w_up, w_down]
    if final_g is not None:
        in_specs.append(pl.BlockSpec((1, d), const))
        args.append(final_g.reshape(1, d))
    return pl.pallas_call(
        functools.partial(_mlp_kernel, final_norm=final_g is not None),
        out_shape=jax.ShapeDtypeStruct((nr * tm, d), F32),
        grid=(nr, ff // tf),
        in_specs=in_specs,
        out_specs=pl.BlockSpec((tm, d), lambda i, f: (i, 0)),
        scratch_shapes=[pltpu.VMEM((tm, d), BF16)],
        compiler_params=_cparams("parallel", "arbitrary"),
        name="mlp",
    )(*args)


def _head_out(o, gate, ng):
    o = _rms_unit(o) * ng
    return gate / (1.0 + jnp.exp(-gate)) * o


def _ret_prompt_kernel(lg_ref, q_ref, k_ref, v_ref, g_ref, ng_ref, y_ref, s_ref, s_scr, *,
                       dk, dv):
    hb = pl.program_id(0)
    c = pl.program_id(1)
    chunk = q_ref.shape[0]
    heads = s_scr.shape[0]

    @pl.when(c == 0)
    def _():
        s_scr[...] = jnp.zeros_like(s_scr)

    ii = lax.broadcasted_iota(jnp.int32, (chunk, chunk), 0)
    jj = lax.broadcasted_iota(jnp.int32, (chunk, chunk), 1)
    diff = (ii - jj).astype(F32)
    idx = lax.broadcasted_iota(jnp.int32, (chunk, 1), 0).astype(F32)

    for hh in range(heads):
        lg = lg_ref[hb * heads + hh]
        q = q_ref[:, hh * dk:(hh + 1) * dk].astype(BF16)
        k = k_ref[:, hh * dk:(hh + 1) * dk]
        v = v_ref[:, hh * dv:(hh + 1) * dv].astype(BF16)
        state = s_scr[hh]

        decay = jnp.where(diff >= 0, jnp.exp(lg * jnp.maximum(diff, 0.0)), 0.0)
        scores = _dot_nt(q, k.astype(BF16)) * decay
        inner = _dot(scores.astype(BF16), v)
        cross = _dot(q, state.astype(BF16)) * jnp.exp(lg * (idx + 1.0))
        k_scaled = (k * jnp.exp(lg * (chunk - 1.0 - idx))).astype(BF16)
        gamma_c = jnp.exp(lg * jnp.full((1, dv), float(chunk), F32))
        s_scr[hh] = gamma_c * state + _dot_tn(k_scaled, v)

        y_ref[:, hh * dv:(hh + 1) * dv] = _head_out(
            inner + cross, g_ref[:, hh * dv:(hh + 1) * dv],
            ng_ref[:, hh * dv:(hh + 1) * dv]).astype(y_ref.dtype)

    @pl.when(c == pl.num_programs(1) - 1)
    def _():
        s_ref[0, 0] = s_scr[...]


def _ret_prompt(proj, log_gamma, norm_g, *, t, dk, dv, heads_per_step):
    nh = RET_HEADS
    hps = heads_per_step
    assert nh % hps == 0
    nb = nh // hps
    chunk = RET_CHUNK if t % RET_CHUNK == 0 else t
    vb = 2 * nh * dk // (hps * dv)
    return pl.pallas_call(
        functools.partial(_ret_prompt_kernel, dk=dk, dv=dv),
        out_shape=(jax.ShapeDtypeStruct((t, nh * dv), BF16),
                   jax.ShapeDtypeStruct((1, 1, nh, dk, dv), F32)),
        grid_spec=pltpu.PrefetchScalarGridSpec(
            num_scalar_prefetch=0,
            grid=(nb, t // chunk),
            in_specs=[pl.BlockSpec(memory_space=pltpu.SMEM),
                      pl.BlockSpec((chunk, hps * dk), lambda h, c: (c, h)),
                      pl.BlockSpec((chunk, hps * dk), lambda h, c: (c, nb + h)),
                      pl.BlockSpec((chunk, hps * dv), lambda h, c: (c, vb + h)),
                      pl.BlockSpec((chunk, hps * dv), lambda h, c: (c, vb + nb + h)),
                      pl.BlockSpec((1, hps * dv), lambda h, c: (0, h))],
            out_specs=(pl.BlockSpec((chunk, hps * dv), lambda h, c: (c, h)),
                       pl.BlockSpec((1, 1, hps, dk, dv), lambda h, c: (0, 0, h, 0, 0))),
            scratch_shapes=[pltpu.VMEM((hps, dk, dv), F32)]),
        compiler_params=_cparams("parallel", "arbitrary"),
        name="ret_prompt",
    )(log_gamma, proj, proj, proj, proj, norm_g.reshape(1, nh * dv))


def _ret_sample_kernel(lg_ref, q_ref, k_ref, v_ref, g_ref, ng_ref, s0_ref, y_ref, s1_ref,
                       kpad_scr, vpad_scr, *, n_tok, dk, dv):
    rows = q_ref.shape[0]
    kpad_scr[...] = jnp.zeros_like(kpad_scr)
    vpad_scr[...] = jnp.zeros_like(vpad_scr)
    idx = lax.broadcasted_iota(jnp.int32, (rows, 1), 0).astype(F32)
    for h in range(RET_HEADS):
        lg = lg_ref[h]
        q = q_ref[:, h * dk:(h + 1) * dk]
        k = k_ref[:, h * dk:(h + 1) * dk]
        v = v_ref[:, h * dv:(h + 1) * dv]
        state = s0_ref[0, 0, h]

        cross = _dot(q.astype(BF16), state.astype(BF16)) * jnp.exp(lg * (idx + 1.0))
        inner = jnp.zeros((rows, dv), F32)
        for j in range(n_tok):
            s_j = jnp.sum(q * k[j:j + 1, :], axis=1, keepdims=True)
            d_j = jnp.where(idx >= j, jnp.exp(lg * jnp.maximum(idx - j, 0.0)), 0.0)
            inner = inner + (s_j * d_j) * v[j:j + 1, :]

        kpad_scr[0:rows, :] = k * jnp.exp(lg * (n_tok - 1.0 - idx))
        vpad_scr[0:rows, :] = v
        upd = _dot_tn(kpad_scr[...].astype(BF16), vpad_scr[...].astype(BF16))
        gamma_c = jnp.exp(lg * jnp.full((1, dv), float(n_tok), F32))
        s1_ref[0, 0, h] = gamma_c * state + upd

        y_ref[:, h * dv:(h + 1) * dv] = _head_out(
            inner + cross, g_ref[:, h * dv:(h + 1) * dv], ng_ref[:, h * dv:(h + 1) * dv])


def _ret_sample(proj, state, log_gamma, norm_g, *, row0, n_seq, n_tok, dk, dv):
    nh = RET_HEADS
    rows = SAMPLE_ROWS
    rb = row0 // rows
    qw = nh * dk
    vw = nh * dv
    return pl.pallas_call(
        functools.partial(_ret_sample_kernel, n_tok=n_tok, dk=dk, dv=dv),
        out_shape=(jax.ShapeDtypeStruct((n_seq * rows, vw), F32),
                   jax.ShapeDtypeStruct(state.shape, F32)),
        grid_spec=pltpu.PrefetchScalarGridSpec(
            num_scalar_prefetch=0,
            grid=(n_seq,),
            in_specs=[pl.BlockSpec(memory_space=pltpu.SMEM),
                      pl.BlockSpec((rows, qw), lambda b: (rb + b, 0)),
                      pl.BlockSpec((rows, qw), lambda b: (rb + b, 1)),
                      pl.BlockSpec((rows, vw), lambda b: (rb + b, 2 * qw // vw)),
                      pl.BlockSpec((rows, vw), lambda b: (rb + b, 2 * qw // vw + 1)),
                      pl.BlockSpec((1, vw), lambda b: (0, 0)),
                      pl.BlockSpec((1, 1, nh, dk, dv), lambda b: (0, b, 0, 0, 0))],
            out_specs=(pl.BlockSpec((rows, vw), lambda b: (b, 0)),
                       pl.BlockSpec((1, 1, nh, dk, dv), lambda b: (0, b, 0, 0, 0))),
            scratch_shapes=[pltpu.VMEM((LANES, dk), F32), pltpu.VMEM((LANES, dv), F32)]),
        compiler_params=_cparams("parallel"),
        name="ret_sample",
    )(log_gamma, proj, proj, proj, proj, norm_g.reshape(1, vw), state)


def _split3(x):
    hi = x.astype(BF16)
    r1 = x - hi.astype(F32)
    mid = r1.astype(BF16)
    lo = (r1 - mid.astype(F32)).astype(BF16)
    return hi, mid, lo


def _cum_bias_kernel(x_ref, qa_ref, ka_ref, carry_scr, *, inv_scale):
    @pl.when(pl.program_id(0) == 0)
    def _():
        carry_scr[...] = jnp.zeros_like(carry_scr)

    x = x_ref[...]
    n = x.shape[0]
    tri = (lax.broadcasted_iota(jnp.int32, (n, n), 0)
           >= lax.broadcasted_iota(jnp.int32, (n, n), 1)).astype(BF16)
    hi, mid, lo = _split3(x)
    c = _dot(tri, hi) + _dot(tri, mid) + _dot(tri, lo) + carry_scr[...]
    carry_scr[...] = c[n - 1:n, :]

    lane = lax.broadcasted_iota(jnp.int32, (n, LANES), 1)
    for h in range(qa_ref.shape[0]):
        b = jnp.broadcast_to(c[:, h:h + 1] * inv_scale, (n, LANES))
        hi, mid, lo = (term.astype(F32) for term in _split3(b))
        qa = jnp.where(lane == 0, hi, jnp.where(lane == 1, mid, jnp.where(
            lane == 2, lo, jnp.where(lane < 6, 1.0, 0.0))))
        ka = jnp.where(lane < 3, 1.0, jnp.where(lane == 3, -hi, jnp.where(
            lane == 4, -mid, jnp.where(lane == 5, -lo, 0.0))))
        qa_ref[h] = qa.astype(BF16)
        ka_ref[h] = ka.astype(BF16)


def _cum_bias(logf, *, rows, tb, scale):
    nh = FOX_HEADS
    out = jax.ShapeDtypeStruct((nh, rows, LANES), BF16)
    spec = pl.BlockSpec((nh, tb, LANES), lambda i: (0, i, 0))
    return pl.pallas_call(
        functools.partial(_cum_bias_kernel, inv_scale=1.0 / scale),
        out_shape=(out, out),
        grid=(rows // tb,),
        in_specs=[pl.BlockSpec((tb, logf.shape[1]), lambda i: (i, 0))],
        out_specs=(spec, spec),
        scratch_shapes=[pltpu.VMEM((1, logf.shape[1]), F32)],
        compiler_params=_cparams("arbitrary"),
        name="cum_bias",
    )(logf)


def _flash_kernel(q_ref, qa_ref, k_ref, ka_ref, v_ref, o_ref, *, scale, tk):
    i = pl.program_id(1)
    tq = q_ref.shape[0]
    heads = qa_ref.shape[0]
    dh = q_ref.shape[1] // heads
    assert tq % tk == 0
    c = scale * math.log2(math.e)
    q = [jnp.concatenate([q_ref[:, g * dh:(g + 1) * dh].astype(BF16), qa_ref[g]], axis=1)
         for g in range(heads)]

    def step(j, carries, diag_offset=None):
        start = pl.multiple_of(j * tk, tk)
        out = []
        for g, (m, l, acc) in enumerate(carries):
            k = jnp.concatenate([k_ref[pl.ds(start, tk), g * dh:(g + 1) * dh],
                                 ka_ref[g, pl.ds(start, tk), :]], axis=1)
            s = _dot_nt(q[g], k)
            if diag_offset is not None:
                qpos = lax.broadcasted_iota(jnp.int32, (tq, tk), 0)
                kpos = lax.broadcasted_iota(jnp.int32, (tq, tk), 1) + diag_offset
                s = jnp.where(kpos <= qpos, s, -jnp.inf)
            m_new = jnp.maximum(m, jnp.max(s, axis=1, keepdims=True))
            p = jnp.exp2(c * (s - m_new))
            alpha = jnp.exp2(c * (m - m_new))
            l = alpha * l + jnp.sum(p, axis=1, keepdims=True)
            acc = alpha * acc + _dot(p.astype(BF16), v_ref[pl.ds(start, tk), g * dh:(g + 1) * dh])
            out.append((m_new, l, acc))
        return tuple(out)

    carries = tuple((jnp.full((tq, 1), -jnp.inf, F32), jnp.zeros((tq, 1), F32),
                     jnp.zeros((tq, dh), F32)) for _ in range(heads))
    n_diag = tq // tk
    carries = lax.fori_loop(0, i * n_diag, lambda j, cr: step(j, cr), carries)
    for d in range(n_diag):
        carries = step(i * n_diag + d, carries, diag_offset=d * tk)
    for g, (m, l, acc) in enumerate(carries):
        o_ref[:, g * dh:(g + 1) * dh] = (acc / l).astype(o_ref.dtype)


def _fox_prompt(q, k16, v16, q_extra, k_extra, *, t, dh, tq, tk, heads_per_step):
    nh = FOX_HEADS
    hps = heads_per_step
    assert nh % hps == 0
    return pl.pallas_call(
        functools.partial(_flash_kernel, scale=dh ** -0.5, tk=tk),
        out_shape=jax.ShapeDtypeStruct((t, nh * dh), BF16),
        grid=(nh // hps, t // tq),
        in_specs=[pl.BlockSpec((tq, hps * dh), lambda h, i: (i, h)),
                  pl.BlockSpec((hps, tq, LANES), lambda h, i: (h, i, 0)),
                  pl.BlockSpec((t, hps * dh), lambda h, i: (0, h)),
                  pl.BlockSpec((hps, t, LANES), lambda h, i: (h, 0, 0)),
                  pl.BlockSpec((t, hps * dh), lambda h, i: (0, h))],
        out_specs=pl.BlockSpec((tq, hps * dh), lambda h, i: (i, h)),
        compiler_params=_cparams("parallel", "arbitrary"),
        name="fox_prompt",
    )(q, q_extra, k16, k_extra, v16)


def _head_suffix_scan(x, nh):
    n = x.shape[1]
    lane = lax.broadcasted_iota(jnp.int32, x.shape, 1)
    y = x
    s = nh
    while s < n:
        shifted = pltpu.roll(y, n - s, axis=1)
        y = y + jnp.where(lane + s < n, shifted, 0.0)
        s *= 2
    return y


def _paged_kernel(pt_ref, q_ref, kn_ref, vn_ref, lfn_ref, *refs, n_tok, pages_per_step, scale):
    del pt_ref
    pps = pages_per_step
    k_refs = refs[0:pps]
    v_refs = refs[pps:2 * pps]
    lf_refs = refs[2 * pps:3 * pps]
    o_ref, mask_scr, m_scr, l_scr, acc_scr, carry_scr = refs[3 * pps:]
    nh = FOX_HEADS
    nq = n_tok * nh
    cols = mask_scr.shape[1]
    step = pl.program_id(1)

    r1 = lax.broadcasted_iota(jnp.int32, (nq, LANES), 0)
    c1 = lax.broadcasted_iota(jnp.int32, (nq, LANES), 1)
    onehot = (c1 == jnp.bitwise_and(r1, nh - 1)).astype(F32)

    @pl.when(step == 0)
    def _():
        row = lax.broadcasted_iota(jnp.int32, (nq, cols), 0)
        col = lax.broadcasted_iota(jnp.int32, (nq, cols), 1)
        same_head = jnp.bitwise_and(row, nh - 1) == jnp.bitwise_and(col, nh - 1)
        mask_scr[...] = jnp.where(same_head, 0.0, -jnp.inf)

        q = q_ref[0]
        kn = kn_ref[0]
        vn = vn_ref[0]
        lfn = lfn_ref[...]
        t_of_row = lax.shift_right_logical(lax.broadcasted_iota(jnp.int32, (nq, 1), 0), _log2(nh))
        cn = jnp.zeros((1, LANES), F32)
        s_new = []
        for t in range(n_tok):
            cn = cn + lfn[t:t + 1, :]
            bias = jnp.sum(onehot * cn, axis=1, keepdims=True)
            k_t = jnp.concatenate([kn[t * nh:(t + 1) * nh, :]] * n_tok, axis=0)
            s_t = jnp.sum(q * k_t, axis=1, keepdims=True) * scale - bias
            s_new.append(jnp.where(t_of_row >= t, s_t, -jnp.inf))
        m0 = s_new[0]
        for t in range(1, n_tok):
            m0 = jnp.maximum(m0, s_new[t])
        l0 = jnp.zeros((nq, 1), F32)
        acc0 = jnp.zeros(acc_scr.shape, F32)
        for t in range(n_tok):
            p_t = jnp.exp(s_new[t] - m0)
            l0 = l0 + p_t
            acc0 = acc0 + p_t * jnp.concatenate([vn[t * nh:(t + 1) * nh, :]] * n_tok, axis=0)
        m_scr[...] = m0
        l_scr[...] = l0
        acc_scr[...] = acc0
        carry_scr[...] = jnp.zeros_like(carry_scr)

    q16 = q_ref[0].astype(BF16)
    carry = carry_scr[...]
    m_prev = m_scr[...]
    m_new = m_prev
    s_pages = []
    shifts = []
    for r in range(pps):
        lf = lf_refs[r][0]
        incl = _head_suffix_scan(lf, nh)
        bias = mask_scr[...] + (incl - lf)
        s = _dot_nt(q16, k_refs[r][0].astype(BF16)) * scale + bias
        m_new = jnp.maximum(m_new, jnp.max(s, axis=1, keepdims=True) + carry)
        s_pages.append(s)
        shifts.append(carry)
        carry = carry + jnp.sum(onehot * incl[:, 0:LANES], axis=1, keepdims=True)
    carry_scr[...] = carry

    alpha = jnp.exp(m_prev - m_new)
    l_new = alpha * l_scr[...]
    acc = alpha * acc_scr[...]
    for r in range(pps):
        p = jnp.exp(s_pages[r] - (m_new - shifts[r]))
        l_new = l_new + jnp.sum(p, axis=1, keepdims=True)
        acc = acc + _dot(p.astype(BF16), v_refs[r][0].astype(BF16))
    m_scr[...] = m_new
    l_scr[...] = l_new
    acc_scr[...] = acc

    @pl.when(step == pl.num_programs(1) - 1)
    def _():
        o_ref[0] = acc / l_new


def _fox_paged(q, k_new, v_new, logf_new, cache_k, cache_v, cache_lf, page_table, *,
               n_tok, pages_per_step):
    nh = FOX_HEADS
    n_seq, n_pages = page_table.shape
    _, nq, dh = q.shape
    cols = PAGE_SIZE * nh
    pps = pages_per_step
    assert n_pages % pps == 0 and nq == n_tok * nh

    def tok(b, s, pt):
        return (b, 0, 0)

    def page(r):
        return lambda b, s, pt: (pt[b, n_pages - 1 - (s * pps + r)], 0, 0)

    in_specs = [pl.BlockSpec((1, nq, dh), tok)] * 3
    in_specs += [pl.BlockSpec((SAMPLE_ROWS, LANES), lambda b, s, pt: (b, 0))]
    in_specs += [pl.BlockSpec((1, cols, dh), page(r)) for r in range(pps)] * 2
    in_specs += [pl.BlockSpec((1, 1, cols), page(r)) for r in range(pps)]
    args = [q, k_new, v_new, logf_new] + [cache_k] * pps + [cache_v] * pps + [cache_lf] * pps
    return pl.pallas_call(
        functools.partial(_paged_kernel, n_tok=n_tok, pages_per_step=pps, scale=dh ** -0.5),
        out_shape=jax.ShapeDtypeStruct((n_seq, nq, dh), F32),
        grid_spec=pltpu.PrefetchScalarGridSpec(
            num_scalar_prefetch=1,
            grid=(n_seq, n_pages // pps),
            in_specs=in_specs,
            out_specs=pl.BlockSpec((1, nq, dh), tok),
            scratch_shapes=[pltpu.VMEM((nq, cols), F32),
                            pltpu.VMEM((nq, 1), F32),
                            pltpu.VMEM((nq, 1), F32),
                            pltpu.VMEM((nq, dh), F32),
                            pltpu.VMEM((nq, 1), F32)]),
        compiler_params=_cparams("parallel", "arbitrary"),
        name="fox_paged",
    )(page_table, *args)


def kernel(x_prompt, x_sample, state_ret, cache_k, cache_v, cache_logf, page_table, norm_a_g,
           w_ret_in, ret_norm_g, w_ret_o, norm_kv_g, w_kvf, b_f, norm_b_g, w_fox_q, w_fox_o,
           norm_mlp_g, w_mlp_up, w_mlp_down, norm_f_g):
    n_b, t, d = x_prompt.shape
    n_seq, n_tok, _ = x_sample.shape
    assert n_b == 1 and n_tok <= SAMPLE_ROWS
    n_a = state_ret.shape[0]
    n_layers = w_mlp_up.shape[0]
    assert n_a == 1 and n_layers == 2
    dk, dv = state_ret.shape[3], state_ret.shape[4]
    nh_r = RET_HEADS
    nh_f = FOX_HEADS
    dh = d // nh_f
    past = page_table.shape[1] * PAGE_SIZE
    ms = n_seq * SAMPLE_ROWS
    tm = 1024 if ms % 1024 == 0 else 512
    assert t % tm == 0 and ms % tm == 0
    p_blocks = (0, t // tm)
    s_blocks = (t // tm, ms // tm)

    xs = jnp.pad(x_sample, ((0, 0), (0, SAMPLE_ROWS - n_tok), (0, 0))).reshape(ms, d)
    x = jnp.concatenate([x_prompt.reshape(t, d), xs], axis=0)

    half = dk // 2
    pos = jnp.concatenate([jnp.arange(t, dtype=jnp.int32),
                           jnp.tile(past + jnp.arange(SAMPLE_ROWS, dtype=jnp.int32), n_seq)])
    inv = 1.0 / (ROPE_BASE ** (jnp.arange(half, dtype=F32) / half))
    ang = pos.astype(F32)[:, None] * inv[None, :]
    cos = jnp.cos(ang)
    sin = jnp.sin(ang)
    log_gamma = jnp.log1p(-(2.0 ** (-5.0 - jnp.arange(nh_r, dtype=F32))))

    tn = 512
    qk_w = 2 * nh_r * dk
    proj = _norm_proj(x, norm_a_g[0], w_ret_in[0].astype(BF16), mode="rope", tm=tm, tn=tn,
                      cos=cos, sin=sin, rope_tiles=qk_w // tn, kscale_from=qk_w // (2 * tn))
    y_p, ret_state_prompt = _ret_prompt(proj, log_gamma, ret_norm_g[0], t=t, dk=dk, dv=dv,
                                        heads_per_step=4)
    y_s, ret_state_sample = _ret_sample(proj, state_ret, log_gamma, ret_norm_g[0], row0=t,
                                        n_seq=n_seq, n_tok=n_tok, dk=dk, dv=dv)
    x = _proj_res(y_p, y_s.astype(BF16), w_ret_o[0].astype(BF16), x, tm=tm, tn=512)
    w_up16 = w_mlp_up.astype(BF16)
    w_down16 = w_mlp_down.astype(BF16)
    x = _mlp(x, norm_mlp_g[0], w_up16, w_down16, layer=0, tm=tm, tf=512)

    kw = nh_f * dh
    w_kv = w_kvf[:, :2 * kw].astype(BF16)
    w_f = jnp.pad(w_kvf[:, 2 * kw:], ((0, 0), (0, LANES - nh_f))).astype(BF16)
    b_pad = jnp.pad(b_f, (0, LANES - nh_f)).reshape(1, LANES)
    k_p, k16_p = _norm_proj(x, norm_kv_g, w_kv[:, :kw], mode="dual", tm=tm, tn=1024,
                            row_blocks=p_blocks)
    v_p, v16_p = _norm_proj(x, norm_kv_g, w_kv[:, kw:], mode="dual", tm=tm, tn=1024,
                            row_blocks=p_blocks)
    kv_s = _norm_proj(x, norm_kv_g, w_kv, tm=tm, tn=1024, row_blocks=s_blocks)
    logf = _norm_proj(x, norm_kv_g, w_f, mode="logsig", bias=b_pad, tm=tm, tn=LANES)

    q = _norm_proj(x, norm_b_g[0], w_fox_q[0].astype(BF16), tm=tm, tn=1024)
    q_extra, k_extra = _cum_bias(logf, rows=t, tb=512, scale=dh ** -0.5)
    a_p = _fox_prompt(q, k16_p, v16_p, q_extra, k_extra, t=t, dh=dh,
                      tq=min(t, 1024), tk=min(t, 1024), heads_per_step=2)

    def head_rows(z):
        return z.reshape(n_seq, SAMPLE_ROWS, nh_f, dh)[:, :n_tok].reshape(n_seq, n_tok * nh_f, dh)

    n_phys = cache_k.shape[0]
    a_s = _fox_paged(head_rows(q[t:]), head_rows(kv_s[:, :nh_f * dh]),
                     head_rows(kv_s[:, nh_f * dh:]), logf[t:],
                     cache_k.reshape(n_phys, PAGE_SIZE * nh_f, dh),
                     cache_v.reshape(n_phys, PAGE_SIZE * nh_f, dh),
                     cache_logf.reshape(n_phys, 1, PAGE_SIZE * nh_f), page_table,
                     n_tok=n_tok, pages_per_step=min(8, page_table.shape[1]))
    a_s = jnp.pad(a_s.reshape(n_seq, n_tok, nh_f * dh).astype(BF16),
                  ((0, 0), (0, SAMPLE_ROWS - n_tok), (0, 0))).reshape(ms, nh_f * dh)
    x = _proj_res(a_p, a_s, w_fox_o[0].astype(BF16), x, tm=tm, tn=1024)
    y_prompt = _mlp(x, norm_mlp_g[1], w_up16, w_down16, layer=1, tm=tm, tf=512,
                    row_blocks=p_blocks, final_g=norm_f_g)
    y_sample = _mlp(x, norm_mlp_g[1], w_up16, w_down16, layer=1, tm=tm, tf=512,
                    row_blocks=s_blocks, final_g=norm_f_g)

    def sample_rows(z):
        return z.reshape(n_seq, SAMPLE_ROWS, -1)[:, :n_tok]

    return (y_prompt.reshape(1, t, d),
            sample_rows(y_sample),
            ret_state_prompt,
            k_p.reshape(1, t, nh_f, dh),
            v_p.reshape(1, t, nh_f, dh),
            logf[:t, :nh_f].reshape(1, t, nh_f),
            ret_state_sample,
            sample_rows(kv_s[:, :kw]).reshape(n_seq, n_tok, nh_f, dh),
            sample_rows(kv_s[:, kw:]).reshape(n_seq, n_tok, nh_f, dh),
            sample_rows(logf[t:, :nh_f]))
```

```python
import functools
import math

import jax
import jax.numpy as jnp
from jax import lax
from jax.experimental import pallas as pl
from jax.experimental.pallas import tpu as pltpu

F32 = jnp.float32
BF16 = jnp.bfloat16

EPS = 1e-6
ROPE_BASE = 10000.0
RET_HEADS = 8
RET_CHUNK = 128
FOX_HEADS = 16
PAGE_SIZE = 128
LANES = 128
SAMPLE_ROWS = 8
VMEM_LIMIT = 56 * 1024 * 1024


def _cparams(*semantics):
    return pltpu.CompilerParams(dimension_semantics=semantics, vmem_limit_bytes=VMEM_LIMIT)


def _dot(a, b):
    return jnp.dot(a, b, preferred_element_type=F32)


def _dot_nt(a, b):
    return lax.dot_general(a, b, (((1,), (1,)), ((), ())), preferred_element_type=F32)


def _dot_tn(a, b):
    return lax.dot_general(a, b, (((0,), (0,)), ((), ())), preferred_element_type=F32)


def _log2(n):
    assert n > 0 and n & (n - 1) == 0, n
    return n.bit_length() - 1


def _rms_unit(x):
    return x * lax.rsqrt(jnp.mean(x * x, axis=-1, keepdims=True) + EPS)


def _norm_proj_kernel(*refs, mode, rope_tiles, kscale_from):
    if mode == "rope":
        x_ref, g_ref, w_ref, cos_ref, sin_ref, o_ref, h_scr = refs
    elif mode == "logsig":
        x_ref, g_ref, w_ref, b_ref, o_ref, h_scr = refs
    elif mode == "dual":
        x_ref, g_ref, w_ref, o_ref, o16_ref, h_scr = refs
    else:
        x_ref, g_ref, w_ref, o_ref, h_scr = refs
    j = pl.program_id(1)

    @pl.when(j == 0)
    def _():
        h_scr[...] = (_rms_unit(x_ref[...]) * g_ref[...]).astype(BF16)

    acc = _dot(h_scr[...], w_ref[...])

    if mode == "rope":
        @pl.when(j < rope_tiles)
        def _():
            cos = cos_ref[...]
            sin = sin_ref[...]
            half = cos.shape[1]
            scale = jnp.where(j >= kscale_from, (2 * half) ** -0.5, 1.0).astype(F32)
            for hh in range(acc.shape[1] // (2 * half)):
                lo = hh * 2 * half
                x1 = acc[:, lo:lo + half]
                x2 = acc[:, lo + half:lo + 2 * half]
                o_ref[:, lo:lo + half] = (x1 * cos - x2 * sin) * scale
                o_ref[:, lo + half:lo + 2 * half] = (x1 * sin + x2 * cos) * scale

        @pl.when(j >= rope_tiles)
        def _():
            o_ref[...] = acc
    elif mode == "logsig":
        z = acc + b_ref[...]
        o_ref[...] = jnp.minimum(z, 0.0) - jnp.log1p(jnp.exp(-jnp.abs(z)))
    elif mode == "dual":
        o_ref[...] = acc
        o16_ref[...] = acc.astype(BF16)
    else:
        o_ref[...] = acc.astype(o_ref.dtype)


def _norm_proj(x, g, w, *, mode="plain", out_dtype=F32, tm, tn, row_blocks=None,
               cos=None, sin=None, bias=None, rope_tiles=0, kscale_from=0):
    m, k = x.shape
    n = w.shape[1]
    r0, nr = row_blocks if row_blocks is not None else (0, m // tm)
    tn = min(tn, n)
    assert m % tm == 0 and n % tn == 0
    grid = (nr, n // tn)
    row = lambda i, j: (i + r0, 0)
    in_specs = [pl.BlockSpec((tm, k), row),
                pl.BlockSpec((1, k), lambda i, j: (0, 0)),
                pl.BlockSpec((k, tn), lambda i, j: (0, j))]
    args = [x, g.reshape(1, k), w]
    if mode == "rope":
        in_specs += [pl.BlockSpec((tm, cos.shape[1]), row)] * 2
        args += [cos, sin]
    if mode == "logsig":
        in_specs += [pl.BlockSpec((1, tn), lambda i, j: (0, j))]
        args += [bias]
    out_spec = pl.BlockSpec((tm, tn), lambda i, j: (i, j))
    out_shape = jax.ShapeDtypeStruct((nr * tm, n), out_dtype)
    if mode == "dual":
        out_shape = (out_shape, jax.ShapeDtypeStruct((nr * tm, n), BF16))
        out_spec = (out_spec, out_spec)
    return pl.pallas_call(
        functools.partial(_norm_proj_kernel, mode=mode, rope_tiles=rope_tiles,
                          kscale_from=kscale_from),
        out_shape=out_shape, grid=grid, in_specs=in_specs, out_specs=out_spec,
        scratch_shapes=[pltpu.VMEM((tm, k), BF16)],
        compiler_params=_cparams("parallel", "arbitrary"),
        name="norm_proj_" + mode,
    )(*args)


def _proj_res_kernel(ap_ref, as_ref, w_ref, rp_ref, rs_ref, o_ref, *, n_prompt):
    i = pl.program_id(0)

    @pl.when(i < n_prompt)
    def _():
        o_ref[...] = rp_ref[...] + _dot(ap_ref[...], w_ref[...])

    @pl.when(i >= n_prompt)
    def _():
        o_ref[...] = rs_ref[...] + _dot(as_ref[...], w_ref[...])


def _proj_res(a_prompt, a_sample, w, res_prompt, res_sample, *, tm, tn, res_sample_row0=0):
    t, k = a_prompt.shape
    ms = a_sample.shape[0]
    n = w.shape[1]
    tn = min(tn, n)
    assert t % tm == 0 and ms % tm == 0 and n % tn == 0 and res_sample_row0 % tm == 0
    n_p = t // tm
    last_j = n // tn - 1
    s0 = res_sample_row0 // tm
    prompt_rows = lambda i: jnp.minimum(i, n_p - 1)
    sample_rows = lambda i: jnp.maximum(i - n_p, 0)
    return pl.pallas_call(
        functools.partial(_proj_res_kernel, n_prompt=n_p),
        out_shape=jax.ShapeDtypeStruct((t + ms, n), F32),
        grid=((t + ms) // tm, n // tn),
        in_specs=[pl.BlockSpec((tm, k), lambda i, j: (prompt_rows(i), 0)),
                  pl.BlockSpec((tm, k), lambda i, j: (sample_rows(i), 0)),
                  pl.BlockSpec((k, tn), lambda i, j: (0, j)),
                  pl.BlockSpec((tm, tn),
                               lambda i, j: (prompt_rows(i), jnp.where(i < n_p, j, last_j))),
                  pl.BlockSpec((tm, tn),
                               lambda i, j: (s0 + sample_rows(i), jnp.where(i >= n_p, j, 0)))],
        out_specs=pl.BlockSpec((tm, tn), lambda i, j: (i, j)),
        compiler_params=_cparams("parallel", "arbitrary"),
        name="proj_res",
    )(a_prompt, a_sample, w, res_prompt, res_sample)


def _mlp_kernel(*refs, final_norm):
    if final_norm:
        x_ref, g_ref, wu_ref, wd_ref, gf_ref, o_ref, h_scr = refs
    else:
        x_ref, g_ref, wu_ref, wd_ref, o_ref, h_scr = refs
    f = pl.program_id(1)

    @pl.when(f == 0)
    def _():
        x = x_ref[...]
        h_scr[...] = (_rms_unit(x) * g_ref[...]).astype(BF16)
        o_ref[...] = x

    a = jnp.maximum(_dot(h_scr[...], wu_ref[...]), 0.0)
    o_ref[...] += _dot((a * a).astype(BF16), wd_ref[...])

    if final_norm:
        @pl.when(f == pl.num_programs(1) - 1)
        def _():
            o_ref[...] = _rms_unit(o_ref[...]) * gf_ref[...]


def _mlp(x, g, w_up, w_down, *, layer, tm, tf, row_blocks=None, final_g=None):
    m, d = x.shape
    ff = w_up.shape[2]
    r0, nr = row_blocks if row_blocks is not None else (0, m // tm)
    assert m % tm == 0 and ff % tf == 0
    row = lambda i, f: (i + r0, 0)
    const = lambda i, f: (0, 0)
    in_specs = [pl.BlockSpec((tm, d), row),
                pl.BlockSpec((1, d), const),
                pl.BlockSpec((None, d, tf), lambda i, f: (layer, 0, f)),
                pl.BlockSpec((None, tf, d), lambda i, f: (layer, f, 0))]
    args = [x, g.reshape(1, d), w_up, w_down]
    if final_g is not None:
        in_specs.append(pl.BlockSpec((1, d), const))
        args.append(final_g.reshape(1, d))
    return pl.pallas_call(
        functools.partial(_mlp_kernel, final_norm=final_g is not None),
        out_shape=jax.ShapeDtypeStruct((nr * tm, d), F32),
        grid=(nr, ff // tf),
        in_specs=in_specs,
        out_specs=pl.BlockSpec((tm, d), lambda i, f: (i, 0)),
        scratch_shapes=[pltpu.VMEM((tm, d), BF16)],
        compiler_params=_cparams("parallel", "arbitrary"),
        name="mlp",
    )(*args)


def _head_out(o, gate, ng):
    o = _rms_unit(o) * ng
    return gate / (1.0 + jnp.exp(-gate)) * o


def _ret_prompt_body(hb, c, n_chunks, lg_ref, q_ref, k_ref, v_ref, g_ref, ng_ref, y_ref, s_ref,
                     s_scr, *, dk, dv):
    chunk = q_ref.shape[0]
    heads = s_scr.shape[0]

    @pl.when(c == 0)
    def _():
        s_scr[...] = jnp.zeros_like(s_scr)

    ii = lax.broadcasted_iota(jnp.int32, (chunk, chunk), 0)
    jj = lax.broadcasted_iota(jnp.int32, (chunk, chunk), 1)
    diff = (ii - jj).astype(F32)
    idx = lax.broadcasted_iota(jnp.int32, (chunk, 1), 0).astype(F32)

    for hh in range(heads):
        lg = lg_ref[hb * heads + hh]
        q = q_ref[:, hh * dk:(hh + 1) * dk].astype(BF16)
        k = k_ref[:, hh * dk:(hh + 1) * dk]
        v = v_ref[:, hh * dv:(hh + 1) * dv].astype(BF16)
        state = s_scr[hh]

        decay = jnp.where(diff >= 0, jnp.exp(lg * jnp.maximum(diff, 0.0)), 0.0)
        scores = _dot_nt(q, k.astype(BF16)) * decay
        inner = _dot(scores.astype(BF16), v)
        cross = _dot(q, state.astype(BF16)) * jnp.exp(lg * (idx + 1.0))
        k_scaled = (k * jnp.exp(lg * (chunk - 1.0 - idx))).astype(BF16)
        gamma_c = jnp.exp(lg * jnp.full((1, dv), float(chunk), F32))
        s_scr[hh] = gamma_c * state + _dot_tn(k_scaled, v)

        y_ref[:, hh * dv:(hh + 1) * dv] = _head_out(
            inner + cross, g_ref[:, hh * dv:(hh + 1) * dv],
            ng_ref[:, hh * dv:(hh + 1) * dv]).astype(y_ref.dtype)

    @pl.when(c == n_chunks - 1)
    def _():
        s_ref[0, 0] = s_scr[...]


def _ret_sample_body(lg_ref, q_ref, k_ref, v_ref, g_ref, ng_ref, s0_ref, y_ref, s1_ref,
                     kpad_scr, vpad_scr, *, n_tok, dk, dv):
    rows = q_ref.shape[0]
    kpad_scr[...] = jnp.zeros_like(kpad_scr)
    vpad_scr[...] = jnp.zeros_like(vpad_scr)
    idx = lax.broadcasted_iota(jnp.int32, (rows, 1), 0).astype(F32)
    for h in range(RET_HEADS):
        lg = lg_ref[h]
        q = q_ref[:, h * dk:(h + 1) * dk]
        k = k_ref[:, h * dk:(h + 1) * dk]
        v = v_ref[:, h * dv:(h + 1) * dv]
        state = s0_ref[0, 0, h]

        cross = _dot(q.astype(BF16), state.astype(BF16)) * jnp.exp(lg * (idx + 1.0))
        inner = jnp.zeros((rows, dv), F32)
        for j in range(n_tok):
            s_j = jnp.sum(q * k[j:j + 1, :], axis=1, keepdims=True)
            d_j = jnp.where(idx >= j, jnp.exp(lg * jnp.maximum(idx - j, 0.0)), 0.0)
            inner = inner + (s_j * d_j) * v[j:j + 1, :]

        kpad_scr[0:rows, :] = k * jnp.exp(lg * (n_tok - 1.0 - idx))
        vpad_scr[0:rows, :] = v
        upd = _dot_tn(kpad_scr[...].astype(BF16), vpad_scr[...].astype(BF16))
        gamma_c = jnp.exp(lg * jnp.full((1, dv), float(n_tok), F32))
        s1_ref[0, 0, h] = gamma_c * state + upd

        y_ref[:, h * dv:(h + 1) * dv] = _head_out(
            inner + cross, g_ref[:, h * dv:(h + 1) * dv], ng_ref[:, h * dv:(h + 1) * dv])


def _retention_kernel(lg_ref, pq_ref, pk_ref, pv_ref, pg_ref, png_ref, sq_ref, sk_ref, sv_ref,
                      sg_ref, sng_ref, s0_ref, y_ref, s_ref, ys_ref, s1_ref, s_scr, kpad_scr,
                      vpad_scr, *, dk, dv, n_tok, n_chunks, n_prompt_steps, n_seq):
    s = pl.program_id(0)

    @pl.when(s < n_prompt_steps)
    def _():
        _ret_prompt_body(s // n_chunks, s % n_chunks, n_chunks, lg_ref, pq_ref, pk_ref, pv_ref,
                         pg_ref, png_ref, y_ref, s_ref, s_scr, dk=dk, dv=dv)

    @pl.when(s < n_seq)
    def _():
        _ret_sample_body(lg_ref, sq_ref, sk_ref, sv_ref, sg_ref, sng_ref, s0_ref, ys_ref, s1_ref,
                         kpad_scr, vpad_scr, n_tok=n_tok, dk=dk, dv=dv)


def _retention(proj_p, proj_s, state, log_gamma, norm_g, *, n_tok, dk, dv, heads_per_step):
    t = proj_p.shape[0]
    n_seq = proj_s.shape[0] // SAMPLE_ROWS
    nh = RET_HEADS
    hps = heads_per_step
    assert nh % hps == 0
    nb = nh // hps
    chunk = RET_CHUNK if t % RET_CHUNK == 0 else t
    nc = t // chunk
    nps = nb * nc
    vb = 2 * nh * dk // (hps * dv)
    rows = SAMPLE_ROWS
    rb = 0
    qw = nh * dk
    vw = nh * dv

    def prompt(col0):
        def index(s):
            sp = jnp.minimum(s, nps - 1)
            return (sp % nc, col0 + sp // nc)
        return index

    def sample(col):
        return lambda s: (rb + jnp.minimum(s, n_seq - 1), col)

    def state_index(s):
        return (0, jnp.minimum(s, n_seq - 1), 0, 0, 0)

    ng = norm_g.reshape(1, vw)
    return pl.pallas_call(
        functools.partial(_retention_kernel, dk=dk, dv=dv, n_tok=n_tok, n_chunks=nc,
                          n_prompt_steps=nps, n_seq=n_seq),
        out_shape=(jax.ShapeDtypeStruct((t, vw), BF16),
                   jax.ShapeDtypeStruct((1, 1, nh, dk, dv), F32),
                   jax.ShapeDtypeStruct((n_seq * rows, vw), F32),
                   jax.ShapeDtypeStruct(state.shape, F32)),
        grid_spec=pltpu.PrefetchScalarGridSpec(
            num_scalar_prefetch=0,
            grid=(max(nps, n_seq),),
            in_specs=[pl.BlockSpec(memory_space=pltpu.SMEM),
                      pl.BlockSpec((chunk, hps * dk), prompt(0)),
                      pl.BlockSpec((chunk, hps * dk), prompt(nb)),
                      pl.BlockSpec((chunk, hps * dv), prompt(vb)),
                      pl.BlockSpec((chunk, hps * dv), prompt(vb + nb)),
                      pl.BlockSpec((1, hps * dv),
                                   lambda s: (0, jnp.minimum(s, nps - 1) // nc)),
                      pl.BlockSpec((rows, qw), sample(0)),
                      pl.BlockSpec((rows, qw), sample(1)),
                      pl.BlockSpec((rows, vw), sample(2 * qw // vw)),
                      pl.BlockSpec((rows, vw), sample(2 * qw // vw + 1)),
                      pl.BlockSpec((1, vw), lambda s: (0, 0)),
                      pl.BlockSpec((1, 1, nh, dk, dv), state_index)],
            out_specs=(pl.BlockSpec((chunk, hps * dv), prompt(0)),
                       pl.BlockSpec((1, 1, hps, dk, dv),
                                    lambda s: (0, 0, jnp.minimum(s, nps - 1) // nc, 0, 0)),
                       pl.BlockSpec((rows, vw), lambda s: (jnp.minimum(s, n_seq - 1), 0)),
                       pl.BlockSpec((1, 1, nh, dk, dv), state_index)),
            scratch_shapes=[pltpu.VMEM((hps, dk, dv), F32),
                            pltpu.VMEM((LANES, dk), F32), pltpu.VMEM((LANES, dv), F32)]),
        compiler_params=_cparams("arbitrary"),
        name="retention",
    )(log_gamma, proj_p, proj_p, proj_p, proj_p, ng, proj_s, proj_s, proj_s, proj_s, ng, state)


def _split3(x):
    hi = x.astype(BF16)
    r1 = x - hi.astype(F32)
    mid = r1.astype(BF16)
    lo = (r1 - mid.astype(F32)).astype(BF16)
    return hi, mid, lo


def _cum_bias_kernel(x_ref, qa_ref, ka_ref, carry_scr, *, inv_scale):
    @pl.when(pl.program_id(0) == 0)
    def _():
        carry_scr[...] = jnp.zeros_like(carry_scr)

    x = x_ref[...]
    n = x.shape[0]
    tri = (lax.broadcasted_iota(jnp.int32, (n, n), 0)
           >= lax.broadcasted_iota(jnp.int32, (n, n), 1)).astype(BF16)
    hi, mid, lo = _split3(x)
    c = _dot(tri, hi) + _dot(tri, mid) + _dot(tri, lo) + carry_scr[...]
    carry_scr[...] = c[n - 1:n, :]

    lane = lax.broadcasted_iota(jnp.int32, (n, LANES), 1)
    for h in range(qa_ref.shape[0]):
        b = jnp.broadcast_to(c[:, h:h + 1] * inv_scale, (n, LANES))
        hi, mid, lo = (term.astype(F32) for term in _split3(b))
        qa = jnp.where(lane == 0, hi, jnp.where(lane == 1, mid, jnp.where(
            lane == 2, lo, jnp.where(lane < 6, 1.0, 0.0))))
        ka = jnp.where(lane < 3, 1.0, jnp.where(lane == 3, -hi, jnp.where(
            lane == 4, -mid, jnp.where(lane == 5, -lo, 0.0))))
        qa_ref[h] = qa.astype(BF16)
        ka_ref[h] = ka.astype(BF16)


def _cum_bias(logf, *, rows, tb, scale):
    nh = FOX_HEADS
    out = jax.ShapeDtypeStruct((nh, rows, LANES), BF16)
    spec = pl.BlockSpec((nh, tb, LANES), lambda i: (0, i, 0))
    return pl.pallas_call(
        functools.partial(_cum_bias_kernel, inv_scale=1.0 / scale),
        out_shape=(out, out),
        grid=(rows // tb,),
        in_specs=[pl.BlockSpec((tb, logf.shape[1]), lambda i: (i, 0))],
        out_specs=(spec, spec),
        scratch_shapes=[pltpu.VMEM((1, logf.shape[1]), F32)],
        compiler_params=_cparams("arbitrary"),
        name="cum_bias",
    )(logf)


def _flash_kernel(q_ref, qa_ref, k_ref, ka_ref, v_ref, o_ref, *, scale, tk):
    i = pl.program_id(1)
    tq = q_ref.shape[0]
    heads = qa_ref.shape[0]
    dh = q_ref.shape[1] // heads
    assert tq % tk == 0
    c = scale * math.log2(math.e)
    q = [jnp.concatenate([q_ref[:, g * dh:(g + 1) * dh].astype(BF16), qa_ref[g]], axis=1)
         for g in range(heads)]

    def step(j, carries, diag_offset=None):
        start = pl.multiple_of(j * tk, tk)
        out = []
        for g, (m, l, acc) in enumerate(carries):
            k = jnp.concatenate([k_ref[pl.ds(start, tk), g * dh:(g + 1) * dh],
                                 ka_ref[g, pl.ds(start, tk), :]], axis=1)
            s = _dot_nt(q[g], k)
            if diag_offset is not None:
                qpos = lax.broadcasted_iota(jnp.int32, (tq, tk), 0)
                kpos = lax.broadcasted_iota(jnp.int32, (tq, tk), 1) + diag_offset
                s = jnp.where(kpos <= qpos, s, -jnp.inf)
            m_new = jnp.maximum(m, jnp.max(s, axis=1, keepdims=True))
            p = jnp.exp2(c * (s - m_new))
            alpha = jnp.exp2(c * (m - m_new))
            l = alpha * l + jnp.sum(p, axis=1, keepdims=True)
            acc = alpha * acc + _dot(p.astype(BF16), v_ref[pl.ds(start, tk), g * dh:(g + 1) * dh])
            out.append((m_new, l, acc))
        return tuple(out)

    carries = tuple((jnp.full((tq, 1), -jnp.inf, F32), jnp.zeros((tq, 1), F32),
                     jnp.zeros((tq, dh), F32)) for _ in range(heads))
    n_diag = tq // tk
    carries = lax.fori_loop(0, i * n_diag, lambda j, cr: step(j, cr), carries)
    for d in range(n_diag):
        carries = step(i * n_diag + d, carries, diag_offset=d * tk)
    for g, (m, l, acc) in enumerate(carries):
        o_ref[:, g * dh:(g + 1) * dh] = (acc / l).astype(o_ref.dtype)


def _fox_prompt(q, k16, v16, q_extra, k_extra, *, t, dh, tq, tk, heads_per_step):
    nh = FOX_HEADS
    hps = heads_per_step
    assert nh % hps == 0
    return pl.pallas_call(
        functools.partial(_flash_kernel, scale=dh ** -0.5, tk=tk),
        out_shape=jax.ShapeDtypeStruct((t, nh * dh), BF16),
        grid=(nh // hps, t // tq),
        in_specs=[pl.BlockSpec((tq, hps * dh), lambda h, i: (i, h)),
                  pl.BlockSpec((hps, tq, LANES), lambda h, i: (h, i, 0)),
                  pl.BlockSpec((t, hps * dh), lambda h, i: (0, h)),
                  pl.BlockSpec((hps, t, LANES), lambda h, i: (h, 0, 0)),
                  pl.BlockSpec((t, hps * dh), lambda h, i: (0, h))],
        out_specs=pl.BlockSpec((tq, hps * dh), lambda h, i: (i, h)),
        compiler_params=_cparams("parallel", "arbitrary"),
        name="fox_prompt",
    )(q, q_extra, k16, k_extra, v16)


def _head_suffix_scan(x, nh):
    n = x.shape[1]
    lane = lax.broadcasted_iota(jnp.int32, x.shape, 1)
    y = x
    s = nh
    while s < n:
        shifted = pltpu.roll(y, n - s, axis=1)
        y = y + jnp.where(lane + s < n, shifted, 0.0)
        s *= 2
    return y


def _paged_kernel(pt_ref, q_ref, kn_ref, vn_ref, lfn_ref, *refs, n_tok, pages_per_step, scale):
    del pt_ref
    pps = pages_per_step
    k_refs = refs[0:pps]
    v_refs = refs[pps:2 * pps]
    lf_refs = refs[2 * pps:3 * pps]
    o_ref, mask_scr, m_scr, l_scr, acc_scr, carry_scr = refs[3 * pps:]
    nh = FOX_HEADS
    nq = n_tok * nh
    cols = mask_scr.shape[1]
    step = pl.program_id(1)

    r1 = lax.broadcasted_iota(jnp.int32, (nq, LANES), 0)
    c1 = lax.broadcasted_iota(jnp.int32, (nq, LANES), 1)
    onehot = (c1 == jnp.bitwise_and(r1, nh - 1)).astype(F32)

    @pl.when(step == 0)
    def _():
        row = lax.broadcasted_iota(jnp.int32, (nq, cols), 0)
        col = lax.broadcasted_iota(jnp.int32, (nq, cols), 1)
        same_head = jnp.bitwise_and(row, nh - 1) == jnp.bitwise_and(col, nh - 1)
        mask_scr[...] = jnp.where(same_head, 0.0, -jnp.inf)

        q = q_ref[0]
        kn = kn_ref[0]
        vn = vn_ref[0]
        lfn = lfn_ref[...]
        t_of_row = lax.shift_right_logical(lax.broadcasted_iota(jnp.int32, (nq, 1), 0), _log2(nh))
        cn = jnp.zeros((1, LANES), F32)
        s_new = []
        for t in range(n_tok):
            cn = cn + lfn[t:t + 1, :]
            bias = jnp.sum(onehot * cn, axis=1, keepdims=True)
            k_t = jnp.concatenate([kn[t * nh:(t + 1) * nh, :]] * n_tok, axis=0)
            s_t = jnp.sum(q * k_t, axis=1, keepdims=True) * scale - bias
            s_new.append(jnp.where(t_of_row >= t, s_t, -jnp.inf))
        m0 = s_new[0]
        for t in range(1, n_tok):
            m0 = jnp.maximum(m0, s_new[t])
        l0 = jnp.zeros((nq, 1), F32)
        acc0 = jnp.zeros(acc_scr.shape, F32)
        for t in range(n_tok):
            p_t = jnp.exp(s_new[t] - m0)
            l0 = l0 + p_t
            acc0 = acc0 + p_t * jnp.concatenate([vn[t * nh:(t + 1) * nh, :]] * n_tok, axis=0)
        m_scr[...] = m0
        l_scr[...] = l0
        acc_scr[...] = acc0
        carry_scr[...] = jnp.zeros_like(carry_scr)

    q16 = q_ref[0].astype(BF16)
    carry = carry_scr[...]
    m_prev = m_scr[...]
    m_new = m_prev
    s_pages = []
    shifts = []
    for r in range(pps):
        lf = lf_refs[r][0]
        incl = _head_suffix_scan(lf, nh)
        bias = mask_scr[...] + (incl - lf)
        s = _dot_nt(q16, k_refs[r][0].astype(BF16)) * scale + bias
        m_new = jnp.maximum(m_new, jnp.max(s, axis=1, keepdims=True) + carry)
        s_pages.append(s)
        shifts.append(carry)
        carry = carry + jnp.sum(onehot * incl[:, 0:LANES], axis=1, keepdims=True)
    carry_scr[...] = carry

    alpha = jnp.exp(m_prev - m_new)
    l_new = alpha * l_scr[...]
    acc = alpha * acc_scr[...]
    for r in range(pps):
        p = jnp.exp(s_pages[r] - (m_new - shifts[r]))
        l_new = l_new + jnp.sum(p, axis=1, keepdims=True)
        acc = acc + _dot(p.astype(BF16), v_refs[r][0].astype(BF16))
    m_scr[...] = m_new
    l_scr[...] = l_new
    acc_scr[...] = acc

    @pl.when(step == pl.num_programs(1) - 1)
    def _():
        o_ref[0] = acc / l_new


def _fox_paged(q, k_new, v_new, logf_new, cache_k, cache_v, cache_lf, page_table, *,
               n_tok, pages_per_step):
    nh = FOX_HEADS
    n_seq, n_pages = page_table.shape
    _, nq, dh = q.shape
    cols = PAGE_SIZE * nh
    pps = pages_per_step
    assert n_pages % pps == 0 and nq == n_tok * nh

    def tok(b, s, pt):
        return (b, 0, 0)

    def page(r):
        return lambda b, s, pt: (pt[b, n_pages - 1 - (s * pps + r)], 0, 0)

    in_specs = [pl.BlockSpec((1, nq, dh), tok)] * 3
    in_specs += [pl.BlockSpec((SAMPLE_ROWS, LANES), lambda b, s, pt: (b, 0))]
    in_specs += [pl.BlockSpec((1, cols, dh), page(r)) for r in range(pps)] * 2
    in_specs += [pl.BlockSpec((1, 1, cols), page(r)) for r in range(pps)]
    args = [q, k_new, v_new, logf_new] + [cache_k] * pps + [cache_v] * pps + [cache_lf] * pps
    return pl.pallas_call(
        functools.partial(_paged_kernel, n_tok=n_tok, pages_per_step=pps, scale=dh ** -0.5),
        out_shape=jax.ShapeDtypeStruct((n_seq, nq, dh), F32),
        grid_spec=pltpu.PrefetchScalarGridSpec(
            num_scalar_prefetch=1,
            grid=(n_seq, n_pages // pps),
            in_specs=in_specs,
            out_specs=pl.BlockSpec((1, nq, dh), tok),
            scratch_shapes=[pltpu.VMEM((nq, cols), F32),
                            pltpu.VMEM((nq, 1), F32),
                            pltpu.VMEM((nq, 1), F32),
                            pltpu.VMEM((nq, dh), F32),
                            pltpu.VMEM((nq, 1), F32)]),
        compiler_params=_cparams("parallel", "arbitrary"),
        name="fox_paged",
    )(page_table, *args)


def kernel(x_prompt, x_sample, state_ret, cache_k, cache_v, cache_logf, page_table, norm_a_g,
           w_ret_in, ret_norm_g, w_ret_o, norm_kv_g, w_kvf, b_f, norm_b_g, w_fox_q, w_fox_o,
           norm_mlp_g, w_mlp_up, w_mlp_down, norm_f_g):
    n_b, t, d = x_prompt.shape
    n_seq, n_tok, _ = x_sample.shape
    assert n_b == 1 and n_tok <= SAMPLE_ROWS
    n_a = state_ret.shape[0]
    n_layers = w_mlp_up.shape[0]
    assert n_a == 1 and n_layers == 2
    dk, dv = state_ret.shape[3], state_ret.shape[4]
    nh_r = RET_HEADS
    nh_f = FOX_HEADS
    dh = d // nh_f
    past = page_table.shape[1] * PAGE_SIZE
    ms = n_seq * SAMPLE_ROWS
    tm = 1024 if ms % 1024 == 0 else 512
    assert t % tm == 0 and ms % tm == 0
    p_blocks = (0, t // tm)
    s_blocks = (t // tm, ms // tm)

    xp = x_prompt.reshape(t, d)
    xs = jnp.pad(x_sample, ((0, 0), (0, SAMPLE_ROWS - n_tok), (0, 0))).reshape(ms, d)

    half = dk // 2
    inv = 1.0 / (ROPE_BASE ** (jnp.arange(half, dtype=F32) / half))

    def rope_tables(pos):
        ang = pos.astype(F32)[:, None] * inv[None, :]
        return jnp.cos(ang), jnp.sin(ang)

    cos_p, sin_p = rope_tables(jnp.arange(t, dtype=jnp.int32))
    cos_s, sin_s = rope_tables(jnp.tile(past + jnp.arange(SAMPLE_ROWS, dtype=jnp.int32), n_seq))
    log_gamma = jnp.log1p(-(2.0 ** (-5.0 - jnp.arange(nh_r, dtype=F32))))

    tn = 512
    qk_w = 2 * nh_r * dk
    w_in16 = w_ret_in[0].astype(BF16)
    rope = dict(mode="rope", tm=tm, tn=tn, rope_tiles=qk_w // tn, kscale_from=qk_w // (2 * tn))
    proj_p = _norm_proj(xp, norm_a_g[0], w_in16, cos=cos_p, sin=sin_p, **rope)
    proj_s = _norm_proj(xs, norm_a_g[0], w_in16, cos=cos_s, sin=sin_s, **rope)
    y_p, ret_state_prompt, y_s, ret_state_sample = _retention(
        proj_p, proj_s, state_ret, log_gamma, ret_norm_g[0], n_tok=n_tok, dk=dk, dv=dv,
        heads_per_step=4)
    x = _proj_res(y_p, y_s.astype(BF16), w_ret_o[0].astype(BF16), xp, xs, tm=tm, tn=512)
    w_up16 = w_mlp_up.astype(BF16)
    w_down16 = w_mlp_down.astype(BF16)
    x = _mlp(x, norm_mlp_g[0], w_up16, w_down16, layer=0, tm=tm, tf=512)

    kw = nh_f * dh
    w_kv = w_kvf[:, :2 * kw].astype(BF16)
    w_f = jnp.pad(w_kvf[:, 2 * kw:], ((0, 0), (0, LANES - nh_f))).astype(BF16)
    b_pad = jnp.pad(b_f, (0, LANES - nh_f)).reshape(1, LANES)
    k_p, k16_p = _norm_proj(x, norm_kv_g, w_kv[:, :kw], mode="dual", tm=tm, tn=1024,
                            row_blocks=p_blocks)
    v_p, v16_p = _norm_proj(x, norm_kv_g, w_kv[:, kw:], mode="dual", tm=tm, tn=1024,
                            row_blocks=p_blocks)
    kv_s = _norm_proj(x, norm_kv_g, w_kv, tm=tm, tn=1024, row_blocks=s_blocks)
    logf = _norm_proj(x, norm_kv_g, w_f, mode="logsig", bias=b_pad, tm=tm, tn=LANES)

    q = _norm_proj(x, norm_b_g[0], w_fox_q[0].astype(BF16), tm=tm, tn=1024)
    q_extra, k_extra = _cum_bias(logf, rows=t, tb=512, scale=dh ** -0.5)
    a_p = _fox_prompt(q, k16_p, v16_p, q_extra, k_extra, t=t, dh=dh,
                      tq=min(t, 1024), tk=min(t, 1024), heads_per_step=2)

    def head_rows(z):
        return z.reshape(n_seq, SAMPLE_ROWS, nh_f, dh)[:, :n_tok].reshape(n_seq, n_tok * nh_f, dh)

    n_phys = cache_k.shape[0]
    a_s = _fox_paged(head_rows(q[t:]), head_rows(kv_s[:, :nh_f * dh]),
                     head_rows(kv_s[:, nh_f * dh:]), logf[t:],
                     cache_k.reshape(n_phys, PAGE_SIZE * nh_f, dh),
                     cache_v.reshape(n_phys, PAGE_SIZE * nh_f, dh),
                     cache_logf.reshape(n_phys, 1, PAGE_SIZE * nh_f), page_table,
                     n_tok=n_tok, pages_per_step=min(8, page_table.shape[1]))
    a_s = jnp.pad(a_s.reshape(n_seq, n_tok, nh_f * dh).astype(BF16),
                  ((0, 0), (0, SAMPLE_ROWS - n_tok), (0, 0))).reshape(ms, nh_f * dh)
    x = _proj_res(a_p, a_s, w_fox_o[0].astype(BF16), x, x, res_sample_row0=t, tm=tm, tn=1024)
    y_prompt = _mlp(x, norm_mlp_g[1], w_up16, w_down16, layer=1, tm=tm, tf=512,
                    row_blocks=p_blocks, final_g=norm_f_g)
    y_sample = _mlp(x, norm_mlp_g[1], w_up16, w_down16, layer=1, tm=tm, tf=512,
                    row_blocks=s_blocks, final_g=norm_f_g)

    def sample_rows(z):
        return z.reshape(n_seq, SAMPLE_ROWS, -1)[:, :n_tok]

    return (y_prompt.reshape(1, t, d),
            sample_rows(y_sample),
            ret_state_prompt,
            k_p.reshape(1, t, nh_f, dh),
            v_p.reshape(1, t, nh_f, dh),
            logf[:t, :nh_f].reshape(1, t, nh_f),
            ret_state_sample,
            sample_rows(kv_s[:, :kw]).reshape(n_seq, n_tok, nh_f, dh),
            sample_rows(kv_s[:, kw:]).reshape(n_seq, n_tok, nh_f, dh),
            sample_rows(logf[t:, :nh_f]))
```

```python
import functools
import math

import jax
import jax.numpy as jnp
from jax import lax
from jax.experimental import pallas as pl
from jax.experimental.pallas import tpu as pltpu

F32 = jnp.float32
BF16 = jnp.bfloat16

EPS = 1e-6
ROPE_BASE = 10000.0
RET_HEADS = 8
RET_CHUNK = 128
FOX_HEADS = 16
PAGE_SIZE = 128
LANES = 128
SAMPLE_ROWS = 8
VMEM_LIMIT = 56 * 1024 * 1024


def _cparams(*semantics):
    return pltpu.CompilerParams(dimension_semantics=semantics, vmem_limit_bytes=VMEM_LIMIT)


def _dot(a, b):
    return jnp.dot(a, b, preferred_element_type=F32)


def _dot_nt(a, b):
    return lax.dot_general(a, b, (((1,), (1,)), ((), ())), preferred_element_type=F32)


def _dot_tn(a, b):
    return lax.dot_general(a, b, (((0,), (0,)), ((), ())), preferred_element_type=F32)


def _log2(n):
    assert n > 0 and n & (n - 1) == 0, n
    return n.bit_length() - 1


def _rms_unit(x):
    return x * lax.rsqrt(jnp.mean(x * x, axis=-1, keepdims=True) + EPS)


def _norm_proj_kernel(*refs, mode, rope_tiles, kscale_from):
    if mode == "rope":
        x_ref, g_ref, w_ref, cos_ref, sin_ref, o_ref, h_scr = refs
    elif mode == "logsig":
        x_ref, g_ref, w_ref, b_ref, o_ref, h_scr = refs
    elif mode == "dual":
        x_ref, g_ref, w_ref, o_ref, o16_ref, h_scr = refs
    else:
        x_ref, g_ref, w_ref, o_ref, h_scr = refs
    j = pl.program_id(1)

    @pl.when(j == 0)
    def _():
        h_scr[...] = (_rms_unit(x_ref[...]) * g_ref[...]).astype(BF16)

    acc = _dot(h_scr[...], w_ref[...])

    if mode == "rope":
        @pl.when(j < rope_tiles)
        def _():
            cos = cos_ref[...]
            sin = sin_ref[...]
            half = cos.shape[1]
            scale = jnp.where(j >= kscale_from, (2 * half) ** -0.5, 1.0).astype(F32)
            for hh in range(acc.shape[1] // (2 * half)):
                lo = hh * 2 * half
                x1 = acc[:, lo:lo + half]
                x2 = acc[:, lo + half:lo + 2 * half]
                o_ref[:, lo:lo + half] = (x1 * cos - x2 * sin) * scale
                o_ref[:, lo + half:lo + 2 * half] = (x1 * sin + x2 * cos) * scale

        @pl.when(j >= rope_tiles)
        def _():
            o_ref[...] = acc
    elif mode == "logsig":
        z = acc + b_ref[...]
        o_ref[...] = jnp.minimum(z, 0.0) - jnp.log1p(jnp.exp(-jnp.abs(z)))
    elif mode == "dual":
        o_ref[...] = acc
        o16_ref[...] = acc.astype(BF16)
    else:
        o_ref[...] = acc.astype(o_ref.dtype)


def _norm_proj(x, g, w, *, mode="plain", out_dtype=F32, tm, tn, row_blocks=None, cols=None,
               cos=None, sin=None, bias=None, rope_tiles=0, kscale_from=0):
    m, k = x.shape
    c0, n = cols if cols is not None else (0, w.shape[1])
    r0, nr = row_blocks if row_blocks is not None else (0, m // tm)
    tn = min(tn, n)
    assert m % tm == 0 and n % tn == 0 and c0 % tn == 0
    cb = c0 // tn
    grid = (nr, n // tn)
    row = lambda i, j: (i + r0, 0)
    in_specs = [pl.BlockSpec((tm, k), row),
                pl.BlockSpec((1, k), lambda i, j: (0, 0)),
                pl.BlockSpec((k, tn), lambda i, j: (0, j + cb))]
    args = [x, g.reshape(1, k), w]
    if mode == "rope":
        in_specs += [pl.BlockSpec((tm, cos.shape[1]), row)] * 2
        args += [cos, sin]
    if mode == "logsig":
        in_specs += [pl.BlockSpec((1, tn), lambda i, j: (0, j))]
        args += [bias]
    out_spec = pl.BlockSpec((tm, tn), lambda i, j: (i, j))
    out_shape = jax.ShapeDtypeStruct((nr * tm, n), out_dtype)
    if mode == "dual":
        out_shape = (out_shape, jax.ShapeDtypeStruct((nr * tm, n), BF16))
        out_spec = (out_spec, out_spec)
    return pl.pallas_call(
        functools.partial(_norm_proj_kernel, mode=mode, rope_tiles=rope_tiles,
                          kscale_from=kscale_from),
        out_shape=out_shape, grid=grid, in_specs=in_specs, out_specs=out_spec,
        scratch_shapes=[pltpu.VMEM((tm, k), BF16)],
        compiler_params=_cparams("parallel", "arbitrary"),
        name="norm_proj_" + mode,
    )(*args)


def _proj_res_kernel(ap_ref, as_ref, w_ref, rp_ref, rs_ref, o_ref, *, n_prompt):
    i = pl.program_id(0)

    @pl.when(i < n_prompt)
    def _():
        o_ref[...] = rp_ref[...] + _dot(ap_ref[...], w_ref[...])

    @pl.when(i >= n_prompt)
    def _():
        o_ref[...] = rs_ref[...] + _dot(as_ref[...], w_ref[...])


def _proj_res(a_prompt, a_sample, w, res_prompt, res_sample, *, tm, tn, res_sample_row0=0):
    t, k = a_prompt.shape
    ms = a_sample.shape[0]
    n = w.shape[1]
    tn = min(tn, n)
    assert t % tm == 0 and ms % tm == 0 and n % tn == 0 and res_sample_row0 % tm == 0
    n_p = t // tm
    last_j = n // tn - 1
    s0 = res_sample_row0 // tm
    prompt_rows = lambda i: jnp.minimum(i, n_p - 1)
    sample_rows = lambda i: jnp.maximum(i - n_p, 0)
    return pl.pallas_call(
        functools.partial(_proj_res_kernel, n_prompt=n_p),
        out_shape=jax.ShapeDtypeStruct((t + ms, n), F32),
        grid=((t + ms) // tm, n // tn),
        in_specs=[pl.BlockSpec((tm, k), lambda i, j: (prompt_rows(i), 0)),
                  pl.BlockSpec((tm, k), lambda i, j: (sample_rows(i), 0)),
                  pl.BlockSpec((k, tn), lambda i, j: (0, j)),
                  pl.BlockSpec((tm, tn),
                               lambda i, j: (prompt_rows(i), jnp.where(i < n_p, j, last_j))),
                  pl.BlockSpec((tm, tn),
                               lambda i, j: (s0 + sample_rows(i), jnp.where(i >= n_p, j, 0)))],
        out_specs=pl.BlockSpec((tm, tn), lambda i, j: (i, j)),
        compiler_params=_cparams("parallel", "arbitrary"),
        name="proj_res",
    )(a_prompt, a_sample, w, res_prompt, res_sample)


def _mlp_kernel(*refs, final_norm):
    if final_norm:
        x_ref, g_ref, wu_ref, wd_ref, gf_ref, o_ref, h_scr = refs
    else:
        x_ref, g_ref, wu_ref, wd_ref, o_ref, h_scr = refs
    f = pl.program_id(1)

    @pl.when(f == 0)
    def _():
        x = x_ref[...]
        h_scr[...] = (_rms_unit(x) * g_ref[...]).astype(BF16)
        o_ref[...] = x

    a = jnp.maximum(_dot(h_scr[...], wu_ref[...]), 0.0)
    o_ref[...] += _dot((a * a).astype(BF16), wd_ref[...])

    if final_norm:
        @pl.when(f == pl.num_programs(1) - 1)
        def _():
            o_ref[...] = _rms_unit(o_ref[...]) * gf_ref[...]


def _mlp(x, g, w_up, w_down, *, layer, tm, tf, row_blocks=None, final_g=None):
    m, d = x.shape
    ff = w_up.shape[2]
    r0, nr = row_blocks if row_blocks is not None else (0, m // tm)
    assert m % tm == 0 and ff % tf == 0
    row = lambda i, f: (i + r0, 0)
    const = lambda i, f: (0, 0)
    in_specs = [pl.BlockSpec((tm, d), row),
                pl.BlockSpec((1, d), const),
                pl.BlockSpec((None, d, tf), lambda i, f: (layer, 0, f)),
                pl.BlockSpec((None, tf, d), lambda i, f: (layer, f, 0))]
    args = [x, g.reshape(1, d), w_up, w_down]
    if final_g is not None:
        in_specs.append(pl.BlockSpec((1, d), const))
        args.append(final_g.reshape(1, d))
    return pl.pallas_call(
        functools.partial(_mlp_kernel, final_norm=final_g is not None),
        out_shape=jax.ShapeDtypeStruct((nr * tm, d), F32),
        grid=(nr, ff // tf),
        in_specs=in_specs,
        out_specs=pl.BlockSpec((tm, d), lambda i, f: (i, 0)),
        scratch_shapes=[pltpu.VMEM((tm, d), BF16)],
        compiler_params=_cparams("parallel", "arbitrary"),
        name="mlp",
    )(*args)


def _head_out(o, gate, ng):
    o = _rms_unit(o) * ng
    return gate / (1.0 + jnp.exp(-gate)) * o


def _ret_prompt_body(hb, c, n_chunks, lg_ref, q_ref, k_ref, v_ref, g_ref, ng_ref, y_ref, s_ref,
                     s_scr, *, dk, dv):
    chunk = q_ref.shape[0]
    heads = s_scr.shape[0]

    @pl.when(c == 0)
    def _():
        s_scr[...] = jnp.zeros_like(s_scr)

    ii = lax.broadcasted_iota(jnp.int32, (chunk, chunk), 0)
    jj = lax.broadcasted_iota(jnp.int32, (chunk, chunk), 1)
    diff = (ii - jj).astype(F32)
    idx = lax.broadcasted_iota(jnp.int32, (chunk, 1), 0).astype(F32)

    for hh in range(heads):
        lg = lg_ref[hb * heads + hh]
        q = q_ref[:, hh * dk:(hh + 1) * dk].astype(BF16)
        k = k_ref[:, hh * dk:(hh + 1) * dk]
        v = v_ref[:, hh * dv:(hh + 1) * dv].astype(BF16)
        state = s_scr[hh]

        decay = jnp.where(diff >= 0, jnp.exp(lg * jnp.maximum(diff, 0.0)), 0.0)
        scores = _dot_nt(q, k.astype(BF16)) * decay
        inner = _dot(scores.astype(BF16), v)
        cross = _dot(q, state.astype(BF16)) * jnp.exp(lg * (idx + 1.0))
        k_scaled = (k * jnp.exp(lg * (chunk - 1.0 - idx))).astype(BF16)
        gamma_c = jnp.exp(lg * jnp.full((1, dv), float(chunk), F32))
        s_scr[hh] = gamma_c * state + _dot_tn(k_scaled, v)

        y_ref[:, hh * dv:(hh + 1) * dv] = _head_out(
            inner + cross, g_ref[:, hh * dv:(hh + 1) * dv],
            ng_ref[:, hh * dv:(hh + 1) * dv]).astype(y_ref.dtype)

    @pl.when(c == n_chunks - 1)
    def _():
        s_ref[0, 0] = s_scr[...]


def _ret_sample_body(lg_ref, q_ref, k_ref, v_ref, g_ref, ng_ref, s0_ref, y_ref, s1_ref,
                     kpad_scr, vpad_scr, *, n_tok, dk, dv):
    rows = q_ref.shape[0]
    kpad_scr[...] = jnp.zeros_like(kpad_scr)
    vpad_scr[...] = jnp.zeros_like(vpad_scr)
    idx = lax.broadcasted_iota(jnp.int32, (rows, 1), 0).astype(F32)
    for h in range(RET_HEADS):
        lg = lg_ref[h]
        q = q_ref[:, h * dk:(h + 1) * dk]
        k = k_ref[:, h * dk:(h + 1) * dk]
        v = v_ref[:, h * dv:(h + 1) * dv]
        state = s0_ref[0, 0, h]

        cross = _dot(q.astype(BF16), state.astype(BF16)) * jnp.exp(lg * (idx + 1.0))
        inner = jnp.zeros((rows, dv), F32)
        for j in range(n_tok):
            s_j = jnp.sum(q * k[j:j + 1, :], axis=1, keepdims=True)
            d_j = jnp.where(idx >= j, jnp.exp(lg * jnp.maximum(idx - j, 0.0)), 0.0)
            inner = inner + (s_j * d_j) * v[j:j + 1, :]

        kpad_scr[0:rows, :] = k * jnp.exp(lg * (n_tok - 1.0 - idx))
        vpad_scr[0:rows, :] = v
        upd = _dot_tn(kpad_scr[...].astype(BF16), vpad_scr[...].astype(BF16))
        gamma_c = jnp.exp(lg * jnp.full((1, dv), float(n_tok), F32))
        s1_ref[0, 0, h] = gamma_c * state + upd

        y_ref[:, h * dv:(h + 1) * dv] = _head_out(
            inner + cross, g_ref[:, h * dv:(h + 1) * dv], ng_ref[:, h * dv:(h + 1) * dv])


def _retention_kernel(lg_ref, pq_ref, pk_ref, pv_ref, pg_ref, png_ref, sq_ref, sk_ref, sv_ref,
                      sg_ref, sng_ref, s0_ref, y_ref, s_ref, ys_ref, s1_ref, s_scr, kpad_scr,
                      vpad_scr, *, dk, dv, n_tok, n_chunks, n_prompt_steps, n_seq):
    s = pl.program_id(0)

    @pl.when(s < n_prompt_steps)
    def _():
        _ret_prompt_body(s // n_chunks, s % n_chunks, n_chunks, lg_ref, pq_ref, pk_ref, pv_ref,
                         pg_ref, png_ref, y_ref, s_ref, s_scr, dk=dk, dv=dv)

    @pl.when(s < n_seq)
    def _():
        _ret_sample_body(lg_ref, sq_ref, sk_ref, sv_ref, sg_ref, sng_ref, s0_ref, ys_ref, s1_ref,
                         kpad_scr, vpad_scr, n_tok=n_tok, dk=dk, dv=dv)


def _retention(proj_p, proj_s, state, log_gamma, norm_g, *, n_tok, dk, dv, heads_per_step):
    t = proj_p.shape[0]
    n_seq = proj_s.shape[0] // SAMPLE_ROWS
    nh = RET_HEADS
    hps = heads_per_step
    assert nh % hps == 0
    nb = nh // hps
    chunk = RET_CHUNK if t % RET_CHUNK == 0 else t
    nc = t // chunk
    nps = nb * nc
    vb = 2 * nh * dk // (hps * dv)
    rows = SAMPLE_ROWS
    rb = 0
    qw = nh * dk
    vw = nh * dv

    def prompt(col0):
        def index(s):
            sp = jnp.minimum(s, nps - 1)
            return (sp % nc, col0 + sp // nc)
        return index

    def sample(col):
        return lambda s: (rb + jnp.minimum(s, n_seq - 1), col)

    def state_index(s):
        return (0, jnp.minimum(s, n_seq - 1), 0, 0, 0)

    ng = norm_g.reshape(1, vw)
    return pl.pallas_call(
        functools.partial(_retention_kernel, dk=dk, dv=dv, n_tok=n_tok, n_chunks=nc,
                          n_prompt_steps=nps, n_seq=n_seq),
        out_shape=(jax.ShapeDtypeStruct((t, vw), BF16),
                   jax.ShapeDtypeStruct((1, 1, nh, dk, dv), F32),
                   jax.ShapeDtypeStruct((n_seq * rows, vw), F32),
                   jax.ShapeDtypeStruct(state.shape, F32)),
        grid_spec=pltpu.PrefetchScalarGridSpec(
            num_scalar_prefetch=0,
            grid=(max(nps, n_seq),),
            in_specs=[pl.BlockSpec(memory_space=pltpu.SMEM),
                      pl.BlockSpec((chunk, hps * dk), prompt(0)),
                      pl.BlockSpec((chunk, hps * dk), prompt(nb)),
                      pl.BlockSpec((chunk, hps * dv), prompt(vb)),
                      pl.BlockSpec((chunk, hps * dv), prompt(vb + nb)),
                      pl.BlockSpec((1, hps * dv),
                                   lambda s: (0, jnp.minimum(s, nps - 1) // nc)),
                      pl.BlockSpec((rows, qw), sample(0)),
                      pl.BlockSpec((rows, qw), sample(1)),
                      pl.BlockSpec((rows, vw), sample(2 * qw // vw)),
                      pl.BlockSpec((rows, vw), sample(2 * qw // vw + 1)),
                      pl.BlockSpec((1, vw), lambda s: (0, 0)),
                      pl.BlockSpec((1, 1, nh, dk, dv), state_index)],
            out_specs=(pl.BlockSpec((chunk, hps * dv), prompt(0)),
                       pl.BlockSpec((1, 1, hps, dk, dv),
                                    lambda s: (0, 0, jnp.minimum(s, nps - 1) // nc, 0, 0)),
                       pl.BlockSpec((rows, vw), lambda s: (jnp.minimum(s, n_seq - 1), 0)),
                       pl.BlockSpec((1, 1, nh, dk, dv), state_index)),
            scratch_shapes=[pltpu.VMEM((hps, dk, dv), F32),
                            pltpu.VMEM((LANES, dk), F32), pltpu.VMEM((LANES, dv), F32)]),
        compiler_params=_cparams("arbitrary"),
        name="retention",
    )(log_gamma, proj_p, proj_p, proj_p, proj_p, ng, proj_s, proj_s, proj_s, proj_s, ng, state)


def _split3(x):
    hi = x.astype(BF16)
    r1 = x - hi.astype(F32)
    mid = r1.astype(BF16)
    lo = (r1 - mid.astype(F32)).astype(BF16)
    return hi, mid, lo


def _cum_bias_kernel(x_ref, qa_ref, ka_ref, carry_scr, *, inv_scale):
    @pl.when(pl.program_id(0) == 0)
    def _():
        carry_scr[...] = jnp.zeros_like(carry_scr)

    x = x_ref[...]
    n = x.shape[0]
    tri = (lax.broadcasted_iota(jnp.int32, (n, n), 0)
           >= lax.broadcasted_iota(jnp.int32, (n, n), 1)).astype(BF16)
    hi, mid, lo = _split3(x)
    c = _dot(tri, hi) + _dot(tri, mid) + _dot(tri, lo) + carry_scr[...]
    carry_scr[...] = c[n - 1:n, :]

    lane = lax.broadcasted_iota(jnp.int32, (n, LANES), 1)
    for h in range(qa_ref.shape[0]):
        b = jnp.broadcast_to(c[:, h:h + 1] * inv_scale, (n, LANES))
        hi, mid, lo = (term.astype(F32) for term in _split3(b))
        qa = jnp.where(lane == 0, hi, jnp.where(lane == 1, mid, jnp.where(
            lane == 2, lo, jnp.where(lane < 6, 1.0, 0.0))))
        ka = jnp.where(lane < 3, 1.0, jnp.where(lane == 3, -hi, jnp.where(
            lane == 4, -mid, jnp.where(lane == 5, -lo, 0.0))))
        qa_ref[h] = qa.astype(BF16)
        ka_ref[h] = ka.astype(BF16)


def _cum_bias(logf, *, rows, tb, scale):
    nh = FOX_HEADS
    out = jax.ShapeDtypeStruct((nh, rows, LANES), BF16)
    spec = pl.BlockSpec((nh, tb, LANES), lambda i: (0, i, 0))
    return pl.pallas_call(
        functools.partial(_cum_bias_kernel, inv_scale=1.0 / scale),
        out_shape=(out, out),
        grid=(rows // tb,),
        in_specs=[pl.BlockSpec((tb, logf.shape[1]), lambda i: (i, 0))],
        out_specs=(spec, spec),
        scratch_shapes=[pltpu.VMEM((1, logf.shape[1]), F32)],
        compiler_params=_cparams("arbitrary"),
        name="cum_bias",
    )(logf)


def _flash_kernel(q_ref, qa_ref, k_ref, ka_ref, v_ref, o_ref, *, scale, tk):
    i = pl.program_id(1)
    tq = q_ref.shape[0]
    heads = qa_ref.shape[0]
    dh = q_ref.shape[1] // heads
    assert tq == tk
    c = scale * math.log2(math.e)
    q = [jnp.concatenate([q_ref[:, g * dh:(g + 1) * dh].astype(BF16), qa_ref[g]], axis=1)
         for g in range(heads)]

    def step(j, carries, masked):
        start = pl.multiple_of(j * tk, tk)
        out = []
        for g, (m, l, acc) in enumerate(carries):
            k = jnp.concatenate([k_ref[pl.ds(start, tk), g * dh:(g + 1) * dh],
                                 ka_ref[g, pl.ds(start, tk), :]], axis=1)
            s = _dot_nt(q[g], k)
            if masked:
                qpos = lax.broadcasted_iota(jnp.int32, (tq, tk), 0)
                kpos = lax.broadcasted_iota(jnp.int32, (tq, tk), 1)
                s = jnp.where(kpos <= qpos, s, -jnp.inf)
            m_new = jnp.maximum(m, jnp.max(s, axis=1, keepdims=True))
            p = jnp.exp2(c * (s - m_new))
            alpha = jnp.exp2(c * (m - m_new))
            l = alpha * l + jnp.sum(p, axis=1, keepdims=True)
            acc = alpha * acc + _dot(p.astype(BF16), v_ref[pl.ds(start, tk), g * dh:(g + 1) * dh])
            out.append((m_new, l, acc))
        return tuple(out)

    carries = tuple((jnp.full((tq, 1), -jnp.inf, F32), jnp.zeros((tq, 1), F32),
                     jnp.zeros((tq, dh), F32)) for _ in range(heads))
    carries = lax.fori_loop(0, i, lambda j, cr: step(j, cr, False), carries)
    carries = step(i, carries, True)
    for g, (m, l, acc) in enumerate(carries):
        o_ref[:, g * dh:(g + 1) * dh] = (acc / l).astype(o_ref.dtype)


def _fox_prompt(q, k16, v16, q_extra, k_extra, *, t, dh, tq, tk, heads_per_step):
    nh = FOX_HEADS
    hps = heads_per_step
    assert nh % hps == 0
    return pl.pallas_call(
        functools.partial(_flash_kernel, scale=dh ** -0.5, tk=tk),
        out_shape=jax.ShapeDtypeStruct((t, nh * dh), BF16),
        grid=(nh // hps, t // tq),
        in_specs=[pl.BlockSpec((tq, hps * dh), lambda h, i: (i, h)),
                  pl.BlockSpec((hps, tq, LANES), lambda h, i: (h, i, 0)),
                  pl.BlockSpec((t, hps * dh), lambda h, i: (0, h)),
                  pl.BlockSpec((hps, t, LANES), lambda h, i: (h, 0, 0)),
                  pl.BlockSpec((t, hps * dh), lambda h, i: (0, h))],
        out_specs=pl.BlockSpec((tq, hps * dh), lambda h, i: (i, h)),
        compiler_params=_cparams("parallel", "arbitrary"),
        name="fox_prompt",
    )(q, q_extra, k16, k_extra, v16)


def _head_suffix_scan(x, nh):
    n = x.shape[1]
    lane = lax.broadcasted_iota(jnp.int32, x.shape, 1)
    y = x
    s = nh
    while s < n:
        shifted = pltpu.roll(y, n - s, axis=1)
        y = y + jnp.where(lane + s < n, shifted, 0.0)
        s *= 2
    return y


def _paged_kernel(pt_ref, q_ref, kn_ref, vn_ref, lfn_ref, *refs, n_tok, pages_per_step, scale):
    del pt_ref
    pps = pages_per_step
    k_refs = refs[0:pps]
    v_refs = refs[pps:2 * pps]
    lf_refs = refs[2 * pps:3 * pps]
    o_ref, mask_scr, m_scr, l_scr, acc_scr, carry_scr = refs[3 * pps:]
    nh = FOX_HEADS
    nq = n_tok * nh
    cols = mask_scr.shape[1]
    step = pl.program_id(1)

    r1 = lax.broadcasted_iota(jnp.int32, (nq, LANES), 0)
    c1 = lax.broadcasted_iota(jnp.int32, (nq, LANES), 1)
    onehot = (c1 == jnp.bitwise_and(r1, nh - 1)).astype(F32)

    @pl.when(step == 0)
    def _():
        row = lax.broadcasted_iota(jnp.int32, (nq, cols), 0)
        col = lax.broadcasted_iota(jnp.int32, (nq, cols), 1)
        same_head = jnp.bitwise_and(row, nh - 1) == jnp.bitwise_and(col, nh - 1)
        mask_scr[...] = jnp.where(same_head, 0.0, -jnp.inf)

        q = q_ref[0]
        kn = kn_ref[0]
        vn = vn_ref[0]
        lfn = lfn_ref[...]
        t_of_row = lax.shift_right_logical(lax.broadcasted_iota(jnp.int32, (nq, 1), 0), _log2(nh))
        cn = jnp.zeros((1, LANES), F32)
        s_new = []
        for t in range(n_tok):
            cn = cn + lfn[t:t + 1, :]
            bias = jnp.sum(onehot * cn, axis=1, keepdims=True)
            k_t = jnp.concatenate([kn[t * nh:(t + 1) * nh, :]] * n_tok, axis=0)
            s_t = jnp.sum(q * k_t, axis=1, keepdims=True) * scale - bias
            s_new.append(jnp.where(t_of_row >= t, s_t, -jnp.inf))
        m0 = s_new[0]
        for t in range(1, n_tok):
            m0 = jnp.maximum(m0, s_new[t])
        l0 = jnp.zeros((nq, 1), F32)
        acc0 = jnp.zeros(acc_scr.shape, F32)
        for t in range(n_tok):
            p_t = jnp.exp(s_new[t] - m0)
            l0 = l0 + p_t
            acc0 = acc0 + p_t * jnp.concatenate([vn[t * nh:(t + 1) * nh, :]] * n_tok, axis=0)
        m_scr[...] = m0
        l_scr[...] = l0
        acc_scr[...] = acc0
        carry_scr[...] = jnp.zeros_like(carry_scr)

    q16 = q_ref[0].astype(BF16)
    carry = carry_scr[...]
    m_prev = m_scr[...]
    m_new = m_prev
    s_pages = []
    shifts = []
    for r in range(pps):
        lf = lf_refs[r][0]
        incl = _head_suffix_scan(lf, nh)
        bias = mask_scr[...] + (incl - lf)
        s = _dot_nt(q16, k_refs[r][0].astype(BF16)) * scale + bias
        m_new = jnp.maximum(m_new, jnp.max(s, axis=1, keepdims=True) + carry)
        s_pages.append(s)
        shifts.append(carry)
        carry = carry + jnp.sum(onehot * incl[:, 0:LANES], axis=1, keepdims=True)
    carry_scr[...] = carry

    alpha = jnp.exp(m_prev - m_new)
    l_new = alpha * l_scr[...]
    acc = alpha * acc_scr[...]
    for r in range(pps):
        p = jnp.exp(s_pages[r] - (m_new - shifts[r]))
        l_new = l_new + jnp.sum(p, axis=1, keepdims=True)
        acc = acc + _dot(p.astype(BF16), v_refs[r][0].astype(BF16))
    m_scr[...] = m_new
    l_scr[...] = l_new
    acc_scr[...] = acc

    @pl.when(step == pl.num_programs(1) - 1)
    def _():
        o_ref[0] = acc / l_new


def _fox_paged(q, k_new, v_new, logf_new, cache_k, cache_v, cache_lf, page_table, *,
               n_tok, pages_per_step):
    nh = FOX_HEADS
    n_seq, n_pages = page_table.shape
    _, nq, dh = q.shape
    cols = PAGE_SIZE * nh
    pps = pages_per_step
    assert n_pages % pps == 0 and nq == n_tok * nh

    def tok(b, s, pt):
        return (b, 0, 0)

    def page(r):
        return lambda b, s, pt: (pt[b, n_pages - 1 - (s * pps + r)], 0, 0)

    in_specs = [pl.BlockSpec((1, nq, dh), tok)] * 3
    in_specs += [pl.BlockSpec((SAMPLE_ROWS, LANES), lambda b, s, pt: (b, 0))]
    in_specs += [pl.BlockSpec((1, cols, dh), page(r)) for r in range(pps)] * 2
    in_specs += [pl.BlockSpec((1, 1, cols), page(r)) for r in range(pps)]
    args = [q, k_new, v_new, logf_new] + [cache_k] * pps + [cache_v] * pps + [cache_lf] * pps
    return pl.pallas_call(
        functools.partial(_paged_kernel, n_tok=n_tok, pages_per_step=pps, scale=dh ** -0.5),
        out_shape=jax.ShapeDtypeStruct((n_seq, nq, dh), F32),
        grid_spec=pltpu.PrefetchScalarGridSpec(
            num_scalar_prefetch=1,
            grid=(n_seq, n_pages // pps),
            in_specs=in_specs,
            out_specs=pl.BlockSpec((1, nq, dh), tok),
            scratch_shapes=[pltpu.VMEM((nq, cols), F32),
                            pltpu.VMEM((nq, 1), F32),
                            pltpu.VMEM((nq, 1), F32),
                            pltpu.VMEM((nq, dh), F32),
                            pltpu.VMEM((nq, 1), F32)]),
        compiler_params=_cparams("parallel", "arbitrary"),
        name="fox_paged",
    )(page_table, *args)


def kernel(x_prompt, x_sample, state_ret, cache_k, cache_v, cache_logf, page_table, norm_a_g,
           w_ret_in, ret_norm_g, w_ret_o, norm_kv_g, w_kvf, b_f, norm_b_g, w_fox_q, w_fox_o,
           norm_mlp_g, w_mlp_up, w_mlp_down, norm_f_g):
    n_b, t, d = x_prompt.shape
    n_seq, n_tok, _ = x_sample.shape
    assert n_b == 1 and n_tok <= SAMPLE_ROWS
    n_a = state_ret.shape[0]
    n_layers = w_mlp_up.shape[0]
    assert n_a == 1 and n_layers == 2
    dk, dv = state_ret.shape[3], state_ret.shape[4]
    nh_r = RET_HEADS
    nh_f = FOX_HEADS
    dh = d // nh_f
    past = page_table.shape[1] * PAGE_SIZE
    ms = n_seq * SAMPLE_ROWS
    tm = 1024 if ms % 1024 == 0 else 512
    assert t % tm == 0 and ms % tm == 0
    p_blocks = (0, t // tm)
    s_blocks = (t // tm, ms // tm)

    xp = x_prompt.reshape(t, d)
    xs = jnp.pad(x_sample, ((0, 0), (0, SAMPLE_ROWS - n_tok), (0, 0))).reshape(ms, d)

    half = dk // 2
    inv = 1.0 / (ROPE_BASE ** (jnp.arange(half, dtype=F32) / half))

    def rope_tables(pos):
        ang = pos.astype(F32)[:, None] * inv[None, :]
        return jnp.cos(ang), jnp.sin(ang)

    cos_p, sin_p = rope_tables(jnp.arange(t, dtype=jnp.int32))
    cos_s, sin_s = rope_tables(jnp.tile(past + jnp.arange(SAMPLE_ROWS, dtype=jnp.int32), n_seq))
    log_gamma = jnp.log1p(-(2.0 ** (-5.0 - jnp.arange(nh_r, dtype=F32))))

    tn = 512
    qk_w = 2 * nh_r * dk
    w_in16 = w_ret_in[0].astype(BF16)
    rope = dict(mode="rope", tm=tm, tn=tn, rope_tiles=qk_w // tn, kscale_from=qk_w // (2 * tn))
    proj_p = _norm_proj(xp, norm_a_g[0], w_in16, cos=cos_p, sin=sin_p, **rope)
    proj_s = _norm_proj(xs, norm_a_g[0], w_in16, cos=cos_s, sin=sin_s, **rope)
    y_p, ret_state_prompt, y_s, ret_state_sample = _retention(
        proj_p, proj_s, state_ret, log_gamma, ret_norm_g[0], n_tok=n_tok, dk=dk, dv=dv,
        heads_per_step=4)
    x = _proj_res(y_p, y_s.astype(BF16), w_ret_o[0].astype(BF16), xp, xs, tm=tm, tn=512)
    w_up16 = w_mlp_up.astype(BF16)
    w_down16 = w_mlp_down.astype(BF16)
    x = _mlp(x, norm_mlp_g[0], w_up16, w_down16, layer=0, tm=tm, tf=512)

    kw = nh_f * dh
    w_kv = w_kvf.astype(BF16)
    w_f = jnp.pad(w_kvf[:, 2 * kw:], ((0, 0), (0, LANES - nh_f))).astype(BF16)
    b_pad = jnp.pad(b_f, (0, LANES - nh_f)).reshape(1, LANES)
    k_p, k16_p = _norm_proj(x, norm_kv_g, w_kv, cols=(0, kw), mode="dual", tm=tm, tn=1024,
                            row_blocks=p_blocks)
    v_p, v16_p = _norm_proj(x, norm_kv_g, w_kv, cols=(kw, kw), mode="dual", tm=tm, tn=1024,
                            row_blocks=p_blocks)
    kv_s = _norm_proj(x, norm_kv_g, w_kv, cols=(0, 2 * kw), tm=tm, tn=1024,
                      row_blocks=s_blocks)
    logf = _norm_proj(x, norm_kv_g, w_f, mode="logsig", bias=b_pad, tm=tm, tn=LANES)

    q = _norm_proj(x, norm_b_g[0], w_fox_q[0].astype(BF16), tm=tm, tn=1024)
    q_extra, k_extra = _cum_bias(logf, rows=t, tb=512, scale=dh ** -0.5)
    a_p = _fox_prompt(q, k16_p, v16_p, q_extra, k_extra, t=t, dh=dh,
                      tq=min(t, 1024), tk=min(t, 1024), heads_per_step=2)

    def head_rows(z):
        return z.reshape(n_seq, SAMPLE_ROWS, nh_f, dh)[:, :n_tok].reshape(n_seq, n_tok * nh_f, dh)

    n_phys = cache_k.shape[0]
    a_s = _fox_paged(head_rows(q[t:]), head_rows(kv_s[:, :nh_f * dh]),
                     head_rows(kv_s[:, nh_f * dh:]), logf[t:],
                     cache_k.reshape(n_phys, PAGE_SIZE * nh_f, dh),
                     cache_v.reshape(n_phys, PAGE_SIZE * nh_f, dh),
                     cache_logf.reshape(n_phys, 1, PAGE_SIZE * nh_f), page_table,
                     n_tok=n_tok, pages_per_step=min(8, page_table.shape[1]))
    a_s = jnp.pad(a_s.reshape(n_seq, n_tok, nh_f * dh).astype(BF16),
                  ((0, 0), (0, SAMPLE_ROWS - n_tok), (0, 0))).reshape(ms, nh_f * dh)
    x = _proj_res(a_p, a_s, w_fox_o[0].astype(BF16), x, x, res_sample_row0=t, tm=tm, tn=1024)
    y_prompt = _mlp(x, norm_mlp_g[1], w_up16, w_down16, layer=1, tm=tm, tf=512,
                    row_blocks=p_blocks, final_g=norm_f_g)
    y_sample = _mlp(x, norm_mlp_g[1], w_up16, w_down16, layer=1, tm=tm, tf=512,
                    row_blocks=s_blocks, final_g=norm_f_g)

    def sample_rows(z):
        return z.reshape(n_seq, SAMPLE_ROWS, -1)[:, :n_tok]

    return (y_prompt.reshape(1, t, d),
            sample_rows(y_sample),
            ret_state_prompt,
            k_p.reshape(1, t, nh_f, dh),
            v_p.reshape(1, t, nh_f, dh),
            logf[:t, :nh_f].reshape(1, t, nh_f),
            ret_state_sample,
            sample_rows(kv_s[:, :kw]).reshape(n_seq, n_tok, nh_f, dh),
            sample_rows(kv_s[:, kw:]).reshape(n_seq, n_tok, nh_f, dh),
            sample_rows(logf[t:, :nh_f]))
```
